```python
import math
import jax, jax.numpy as jnp
from jax import lax
import numpy as np

D_MODEL = 1024
BATCH = 16
SEQ = 2048
DEPTH = 2
DEC_BATCH = 32
DEC_SEQ = 4
PAST_LEN = 16384
PAGE_SIZE = 128

RET_HEADS = 4
RET_DK = 128
RET_DV = 128
RET_WIDTH = RET_HEADS * RET_DV
RET_CHUNK = 128
RET_ROPE_THETA = 10000.0
ATT_HEADS = 4
HEAD_DIM = 128
ATT_WIDTH = ATT_HEADS * HEAD_DIM
ROPE_DIM = HEAD_DIM // 4
ROPE_THETA = 500000.0
IDX_HEADS = 8
IDX_DIM = 64
IDX_ROPE_DIM = IDX_DIM // 4
TOPK_MAX = 256
Q_BLOCK = 128
EPS = 1e-6

PROJ_SIZES = (RET_HEADS * RET_DK, RET_HEADS * RET_DK, RET_WIDTH, RET_WIDTH,
              ATT_WIDTH, ATT_WIDTH, ATT_WIDTH, ATT_WIDTH,
              IDX_HEADS * IDX_DIM, IDX_DIM, IDX_HEADS,
              D_MODEL, D_MODEL)
PROJ_WIDTH = sum(PROJ_SIZES)
PROJ_OFFSETS = tuple(int(v) for v in np.cumsum(PROJ_SIZES)[:-1])

kernel_name = 'hybrid_retention_dsa_gated_decoder_step'


def rms_norm(x, g):
    x32 = x.astype(jnp.float32)
    y = x32 * lax.rsqrt(jnp.mean(x32 * x32, axis=-1, keepdims=True) + EPS)
    return y.astype(x.dtype) * g


def rope(x, pos, rot_dim, theta):
    half = rot_dim // 2
    freqs = jnp.exp(-jnp.arange(half, dtype=jnp.float32) * (2.0 / rot_dim) * math.log(theta))
    ang = pos.astype(jnp.float32)[:, None] * freqs[None, :]
    cos = jnp.cos(ang)[:, None, :].astype(x.dtype)
    sin = jnp.sin(ang)[:, None, :].astype(x.dtype)
    x1, x2 = x[..., :half], x[..., half:rot_dim]
    return jnp.concatenate([x1 * cos - x2 * sin, x1 * sin + x2 * cos, x[..., rot_dim:]], axis=-1)


def ret_log_decay():
    return jnp.log1p(-jnp.exp2(-5.0 - jnp.arange(RET_HEADS, dtype=jnp.float32)))


def in_projection(x, pos, norm_g, w_in, q_norm_g, k_norm_g, kidx_norm_g):
    B, T = x.shape[0], x.shape[1]
    h = rms_norm(x, norm_g)
    rq, rk, rv, rg, aq, ak, av, ag, iq, ik, iw, ga, gb = jnp.split(h @ w_in, PROJ_OFFSETS, axis=-1)
    rq = rope(rq.reshape(B, T, RET_HEADS, RET_DK), pos, RET_DK, RET_ROPE_THETA)
    rk = rope(rk.reshape(B, T, RET_HEADS, RET_DK), pos, RET_DK, RET_ROPE_THETA) * (RET_DK ** -0.5)
    rv = rv.reshape(B, T, RET_HEADS, RET_DV)
    aq = rope(rms_norm(aq.reshape(B, T, ATT_HEADS, HEAD_DIM), q_norm_g), pos, ROPE_DIM, ROPE_THETA)
    ak = rope(rms_norm(ak.reshape(B, T, ATT_HEADS, HEAD_DIM), k_norm_g), pos, ROPE_DIM, ROPE_THETA)
    av = av.reshape(B, T, ATT_HEADS, HEAD_DIM)
    iq = rope(iq.reshape(B, T, IDX_HEADS, IDX_DIM), pos, IDX_ROPE_DIM, ROPE_THETA)
    ik = rope(rms_norm(ik, kidx_norm_g)[:, :, None, :], pos, IDX_ROPE_DIM, ROPE_THETA)[:, :, 0, :]
    iw = iw * (IDX_HEADS ** -0.5)
    return rq, rk, rv, rg, aq, ak, av, ag, iq, ik, iw, ga, gb


def retention_chunk(state, q, k, v):
    C = q.shape[2]
    log_g = ret_log_decay()
    i = jnp.arange(C, dtype=jnp.float32)
    diff = i[:, None] - i[None, :]
    dmask = jnp.where(diff >= 0, jnp.exp(log_g[:, None, None] * jnp.maximum(diff, 0.0)), 0.0).astype(q.dtype)
    scores = jnp.einsum('bhid,bhjd->bhij', q, k) * dmask
    inner = jnp.einsum('bhij,bhje->bhie', scores, v)
    q_dec = jnp.exp(log_g[:, None] * (i + 1.0)[None, :]).astype(q.dtype)[..., None]
    cross = jnp.einsum('bhid,bhde->bhie', q * q_dec, state)
    k_dec = jnp.exp(log_g[:, None] * (C - 1.0 - i)[None, :]).astype(q.dtype)[..., None]
    chunk_decay = jnp.exp(log_g * C).astype(state.dtype)[:, None, None]
    new_state = chunk_decay * state + jnp.einsum('bhjd,bhje->bhde', k * k_dec, v)
    return new_state, inner + cross


def retention_prompt(q, k, v):
    B, T, H, dk = q.shape
    nc = T // RET_CHUNK
    to_chunks = lambda a: a.reshape(B, nc, RET_CHUNK, H, a.shape[-1]).transpose(1, 0, 3, 2, 4)
    s0 = jnp.zeros((B, H, dk, v.shape[-1]), q.dtype)
    s, o = lax.scan(lambda st, c: retention_chunk(st, *c), s0, (to_chunks(q), to_chunks(k), to_chunks(v)))
    return o.transpose(1, 0, 3, 2, 4).reshape(B, T, H, v.shape[-1]), s


def retention_sample(state, q, k, v):
    t = lambda a: a.transpose(0, 2, 1, 3)
    s, o = retention_chunk(state, t(q), t(k), t(v))
    return t(o), s


def indexer_scores(iq, iw, ik):
    s = jax.nn.relu(jnp.einsum('bqhd,bsd->bqhs', iq, ik) * (IDX_DIM ** -0.5))
    return jnp.einsum('bqhs,bqh->bqs', s, iw).astype(jnp.float32)


def select_topk(scores, q_pos, n_keys):
    k = min(TOPK_MAX, n_keys // 4)
    causal = jnp.arange(n_keys)[None, None, :] <= q_pos[None, :, None]
    vals, idx = lax.top_k(jnp.where(causal, scores, -jnp.inf), k)
    return idx, jnp.isfinite(vals)


def gather_rows(a, idx):
    return jax.vmap(lambda ab, ib: ab[ib])(a, idx)


def sparse_attend(q, kg, vg, valid):
    s = jnp.einsum('bqhd,bqkhd->bqhk', q, kg).astype(jnp.float32) * (HEAD_DIM ** -0.5)
    s = jnp.where(valid[:, :, None, :], s, -jnp.inf)
    p = jax.nn.softmax(s, axis=-1).astype(vg.dtype)
    return jnp.einsum('bqhk,bqkhd->bqhd', p, vg)


def dsa_prompt(aq, ak, av, iq, iw, ik):
    B, T = aq.shape[0], aq.shape[1]
    nb = T // Q_BLOCK
    blocks = lambda a: a.reshape((B, nb, Q_BLOCK) + a.shape[2:]).swapaxes(0, 1)

    def one_block(args):
        b, qb, iqb, iwb = args
        q_pos = b * Q_BLOCK + jnp.arange(Q_BLOCK)
        idx, valid = select_topk(indexer_scores(iqb, iwb, ik), q_pos, T)
        return sparse_attend(qb, gather_rows(ak, idx), gather_rows(av, idx), valid)

    o = lax.map(one_block, (jnp.arange(nb), blocks(aq), blocks(iq), blocks(iw)))
    return o.swapaxes(0, 1).reshape(B, T, ATT_HEADS, HEAD_DIM)


def dsa_sample(aq, ak, av, iq, iw, ik, cache_k, cache_v, cache_kidx, page_table):
    DB, Ds = aq.shape[0], aq.shape[1]
    P = page_table.shape[1] * PAGE_SIZE
    L = P + Ds
    ik_past = cache_kidx[page_table].reshape(DB, P, IDX_DIM)
    ik_all = jnp.concatenate([ik_past, ik.astype(ik_past.dtype)], axis=1)
    q_pos = P + jnp.arange(Ds)
    idx, valid = select_topk(indexer_scores(iq, iw, ik_all), q_pos, L)
    is_new = idx >= P
    past_idx = jnp.minimum(idx, P - 1)
    phys = jax.vmap(lambda pt, s: pt[s // PAGE_SIZE] * PAGE_SIZE + s % PAGE_SIZE)(page_table, past_idx)
    new_idx = jnp.clip(idx - P, 0, Ds - 1)

    def fetch(pool, new):
        past = pool.reshape((-1,) + pool.shape[2:])[phys]
        return jnp.where(is_new[..., None, None], gather_rows(new, new_idx).astype(past.dtype), past)

    return sparse_attend(aq, fetch(cache_k, ak), fetch(cache_v, av), valid)


def branch_merge(o_ret, o_att, rg, ag, ga, gb, ret_norm_g, w_br_ret, w_br_att, w_out):
    B, T = o_ret.shape[0], o_ret.shape[1]
    o_ret = rms_norm(o_ret, ret_norm_g.reshape(RET_HEADS, RET_DV)).reshape(B, T, RET_WIDTH) * jax.nn.silu(rg)
    o_att = o_att.reshape(B, T, ATT_WIDTH) * jax.nn.silu(ag)
    y = jax.nn.sigmoid(ga) * (o_ret @ w_br_ret) + jax.nn.sigmoid(gb) * (o_att @ w_br_att)
    return y @ w_out


def mixer_layer(x, pos, retention_fn, attention_fn, norm_g, w_in, q_norm_g, k_norm_g, kidx_norm_g,
                ret_norm_g, w_br_ret, w_br_att, w_out):
    rq, rk, rv, rg, aq, ak, av, ag, iq, ik, iw, ga, gb = in_projection(
        x, pos, norm_g, w_in, q_norm_g, k_norm_g, kidx_norm_g)
    o_ret, s_new = retention_fn(rq, rk, rv)
    o_att = attention_fn(aq, ak, av, iq, iw, ik)
    y = x + branch_merge(o_ret, o_att, rg, ag, ga, gb, ret_norm_g, w_br_ret, w_br_att, w_out)
    return y, ak, av, ik, s_new


def setup_inputs(seed: int = 0) -> dict:
    key = jax.random.key(seed)
    ks = jax.random.split(key, 16)
    n_pages = PAST_LEN // PAGE_SIZE
    n_used = DEC_BATCH * n_pages
    n_pool = n_used + n_used // 4
    nrm = jax.random.normal
    page_table = jax.random.permutation(ks[6], n_pool)[:n_used].reshape(DEC_BATCH, n_pages).astype(jnp.int32)
    return {
        'x_prompt': nrm(ks[0], (BATCH, SEQ, D_MODEL), jnp.float32),
        'x_sample': nrm(ks[1], (DEC_BATCH, DEC_SEQ, D_MODEL), jnp.float32),
        'cache_k': nrm(ks[2], (DEPTH, n_pool, PAGE_SIZE, ATT_HEADS, HEAD_DIM), jnp.float32),
        'cache_v': nrm(ks[3], (DEPTH, n_pool, PAGE_SIZE, ATT_HEADS, HEAD_DIM), jnp.float32),
        'cache_kidx': nrm(ks[4], (DEPTH, n_pool, PAGE_SIZE, IDX_DIM), jnp.float32),
        'state_ret': 0.5 * nrm(ks[5], (DEPTH, DEC_BATCH, RET_HEADS, RET_DK, RET_DV), jnp.float32),
        'page_table': page_table,
        'norm_g': 1.0 + 0.02 * nrm(ks[7], (DEPTH, D_MODEL), jnp.float32),
        'w_in': nrm(ks[8], (DEPTH, D_MODEL, PROJ_WIDTH), jnp.float32) * D_MODEL ** -0.5,
        'q_norm_g': 1.0 + 0.02 * nrm(ks[9], (DEPTH, HEAD_DIM), jnp.float32),
        'k_norm_g': 1.0 + 0.02 * nrm(ks[10], (DEPTH, HEAD_DIM), jnp.float32),
        'kidx_norm_g': 1.0 + 0.02 * nrm(ks[11], (DEPTH, IDX_DIM), jnp.float32),
        'ret_norm_g': 1.0 + 0.02 * nrm(ks[12], (DEPTH, RET_WIDTH), jnp.float32),
        'w_br_ret': nrm(ks[13], (DEPTH, RET_WIDTH, D_MODEL), jnp.float32) * RET_WIDTH ** -0.5,
        'w_br_att': nrm(ks[14], (DEPTH, ATT_WIDTH, D_MODEL), jnp.float32) * ATT_WIDTH ** -0.5,
        'w_out': nrm(ks[15], (DEPTH, D_MODEL, D_MODEL), jnp.float32) * D_MODEL ** -0.5,
    }


def reference(x_prompt, x_sample, cache_k, cache_v, cache_kidx, state_ret, page_table,
              norm_g, w_in, q_norm_g, k_norm_g, kidx_norm_g, ret_norm_g, w_br_ret, w_br_att, w_out):
    T = x_prompt.shape[1]
    Ds = x_sample.shape[1]
    P = page_table.shape[1] * PAGE_SIZE
    pos_p = jnp.arange(T)
    pos_s = P + jnp.arange(Ds)
    xp, xs = x_prompt, x_sample
    kp, vp, ip, sp = [], [], [], []
    kd, vd, idd, sd = [], [], [], []
    for l in range(DEPTH):
        w_l = (norm_g[l], w_in[l], q_norm_g[l], k_norm_g[l], kidx_norm_g[l],
               ret_norm_g[l], w_br_ret[l], w_br_att[l], w_out[l])
        xp, k_new, v_new, i_new, s_new = mixer_layer(xp, pos_p, retention_prompt, dsa_prompt, *w_l)
        kp.append(k_new); vp.append(v_new); ip.append(i_new); sp.append(s_new)
        ret_fn = lambda q, k, v, st=state_ret[l]: retention_sample(st, q, k, v)
        att_fn = lambda aq, ak, av, iq, iw, ik, ck=cache_k[l], cv=cache_v[l], ci=cache_kidx[l]: dsa_sample(
            aq, ak, av, iq, iw, ik, ck, cv, ci, page_table)
        xs, k_new, v_new, i_new, s_new = mixer_layer(xs, pos_s, ret_fn, att_fn, *w_l)
        kd.append(k_new); vd.append(v_new); idd.append(i_new); sd.append(s_new)
    return (xp, xs,
            jnp.stack(kp), jnp.stack(vp), jnp.stack(ip), jnp.stack(sp),
            jnp.stack(kd), jnp.stack(vd), jnp.stack(idd), jnp.stack(sd))
```

```python
import functools
import math

import jax
import jax.numpy as jnp
from jax import lax
from jax.experimental import pallas as pl
from jax.experimental.pallas import tpu as pltpu

D_MODEL = 1024
PAGE_SIZE = 128
RET_HEADS = 4
RET_DK = 128
RET_DV = 128
RET_WIDTH = RET_HEADS * RET_DV
RET_CHUNK = 128
RET_ROPE_THETA = 10000.0
ATT_HEADS = 4
HEAD_DIM = 128
ATT_WIDTH = ATT_HEADS * HEAD_DIM
ROPE_DIM = HEAD_DIM // 4
ROPE_THETA = 500000.0
IDX_HEADS = 8
IDX_DIM = 64
IDX_ROPE_DIM = IDX_DIM // 4
TOPK_MAX = 256
Q_BLOCK = 128
EPS = 1e-6

LANES = 128
SUBLANES = 8
VMEM_LIMIT = 56 * 1024 * 1024

_O_RQ, _O_RK, _O_RV, _O_RG = 0, 512, 1024, 1536
_O_AQ, _O_AK, _O_AV, _O_AG = 2048, 2560, 3072, 3584
_O_IQ, _O_IK, _O_IW = 4096, 4608, 4672
_O_GA, _O_GB, _O_END = 4680, 5704, 6728

SAMPLE_ROWS = 16
SAMPLE_Q = 8
MASK_NEG = -1e30
INT_MIN = -(2 ** 31)

f32 = jnp.float32
bf16 = jnp.bfloat16
i32 = jnp.int32

_NT = (((1,), (1,)), ((), ()))


def _cparams(sem):
    return pltpu.CompilerParams(dimension_semantics=sem, vmem_limit_bytes=VMEM_LIMIT)


def _sortable(s):
    b = pltpu.bitcast(s + 0.0, i32)
    return b ^ ((b >> 31) & jnp.int32(0x7FFFFFFF))


_NEG_INF_KEY = -2139095041


def _rope_tables(pos, rot_dim, theta, width):
    half = rot_dim // 2
    freqs = jnp.exp(-jnp.arange(half, dtype=f32) * (2.0 / rot_dim) * math.log(theta))
    ang = pos.astype(f32)[:, None] * freqs[None, :]
    cos, sin = jnp.cos(ang), jnp.sin(ang)
    t = pos.shape[0]
    c = jnp.concatenate([cos, cos, jnp.ones((t, width - rot_dim), f32)], axis=-1)
    s = jnp.concatenate([sin, -sin, jnp.zeros((t, width - rot_dim), f32)], axis=-1)
    return c, s, cos.T, sin.T


def _rope_lanes(x, c, s, half, rot, lane):
    w = x * s
    if rot == LANES:
        return x * c + pltpu.roll(w, half, 1)
    up = pltpu.roll(w, LANES - half, 1)
    dn = pltpu.roll(w, half, 1)
    return x * c + jnp.where(lane < half, up, jnp.where(lane < rot, dn, 0.0))


def _inproj_kernel(x_ref, ng_ref, wr_ref, wa_ref, wiq_ref, wik_ref, qg_ref, kg_ref, ig_ref,
                   rc_ref, rs_ref, ac_ref, as_ref, ic_ref, is_ref, ict_ref, ist_ref,
                   rq_ref, rk_ref, rv_ref, aq_ref, ak_ref, av_ref, iqt_ref, ik_ref, iwt_ref):
    x = x_ref[...]
    tm = x.shape[0]
    ms = jnp.mean(x * x, axis=-1, keepdims=True)
    h = ((x * lax.rsqrt(ms + EPS)) * ng_ref[...]).astype(bf16)
    lane = lax.broadcasted_iota(i32, (tm, LANES), 1)

    r = jnp.dot(h, wr_ref[...], preferred_element_type=f32)
    rc, rs = rc_ref[...], rs_ref[...]
    for hd in range(RET_HEADS):
        sl = slice(hd * RET_DK, (hd + 1) * RET_DK)
        q = r[:, _O_RQ + hd * RET_DK:_O_RQ + (hd + 1) * RET_DK]
        k = r[:, _O_RK + hd * RET_DK:_O_RK + (hd + 1) * RET_DK]
        rq_ref[:, sl] = _rope_lanes(q, rc, rs, RET_DK // 2, RET_DK, lane).astype(bf16)
        rk_ref[:, sl] = (_rope_lanes(k, rc, rs, RET_DK // 2, RET_DK, lane) * (RET_DK ** -0.5)).astype(bf16)
    rv_ref[...] = r[:, _O_RV:_O_RV + RET_WIDTH].astype(bf16)

    a = jnp.dot(h, wa_ref[...], preferred_element_type=f32)
    ac, as_ = ac_ref[...], as_ref[...]
    for hd in range(ATT_HEADS):
        sl = slice(hd * HEAD_DIM, (hd + 1) * HEAD_DIM)
        q = a[:, hd * HEAD_DIM:(hd + 1) * HEAD_DIM]
        k = a[:, ATT_WIDTH + hd * HEAD_DIM:ATT_WIDTH + (hd + 1) * HEAD_DIM]
        q = (q * lax.rsqrt(jnp.mean(q * q, axis=-1, keepdims=True) + EPS)) * qg_ref[...]
        k = (k * lax.rsqrt(jnp.mean(k * k, axis=-1, keepdims=True) + EPS)) * kg_ref[...]
        aq_ref[:, sl] = _rope_lanes(q, ac, as_, ROPE_DIM // 2, ROPE_DIM, lane).astype(bf16)
        ak_ref[:, sl] = _rope_lanes(k, ac, as_, ROPE_DIM // 2, ROPE_DIM, lane)
    av_ref[...] = a[:, 2 * ATT_WIDTH:3 * ATT_WIDTH]

    pk = jnp.dot(h, wik_ref[...], preferred_element_type=f32)
    mk = jnp.sum(pk * pk, axis=-1, keepdims=True) * (1.0 / IDX_DIM)
    pk = (pk * lax.rsqrt(mk + EPS)) * ig_ref[...]
    pk = _rope_lanes(pk, ic_ref[...], is_ref[...], IDX_ROPE_DIM // 2, IDX_ROPE_DIM, lane)
    ik_ref[...] = pk[:, :IDX_DIM]

    t = lax.dot_general(wiq_ref[...], h, _NT, preferred_element_type=f32)
    ct, st = ict_ref[...], ist_ref[...]
    hr = IDX_ROPE_DIM // 2
    for hd in range(IDX_HEADS):
        base = hd * IDX_DIM
        x1 = t[base:base + hr]
        x2 = t[base + hr:base + 2 * hr]
        blk = jnp.concatenate([x1 * ct - x2 * st, x1 * st + x2 * ct, t[base + 2 * hr:base + IDX_DIM]], axis=0)
        iqt_ref[base:base + IDX_DIM, :] = blk.astype(bf16)
    iwt_ref[...] = t[IDX_HEADS * IDX_DIM:IDX_HEADS * IDX_DIM + IDX_HEADS] * (IDX_HEADS ** -0.5)


def _inproj(x2d, w, tabs, tm):
    n = x2d.shape[0]
    rc, rs, ac, as_, ic, is_, ict, ist = tabs
    tt = rc.shape[0]
    nj, nr = tt // tm, n // tt
    row = lambda j, r: (r * nj + j, 0)
    col = lambda j, r: (0, r * nj + j)
    tab = lambda j, r: (j, 0)
    tabt = lambda j, r: (0, j)
    const = lambda j, r: (0, 0)
    full = lambda a: pl.BlockSpec(a.shape, const)
    in_specs = [
        pl.BlockSpec((tm, D_MODEL), row),
        full(w["ng"]), full(w["wr"]), full(w["wa"]), full(w["wiq"]), full(w["wik"]),
        full(w["qg"]), full(w["kg"]), full(w["ig"]),
        pl.BlockSpec((tm, LANES), tab), pl.BlockSpec((tm, LANES), tab),
        pl.BlockSpec((tm, LANES), tab), pl.BlockSpec((tm, LANES), tab),
        pl.BlockSpec((tm, LANES), tab), pl.BlockSpec((tm, LANES), tab),
        pl.BlockSpec((IDX_ROPE_DIM // 2, tm), tabt), pl.BlockSpec((IDX_ROPE_DIM // 2, tm), tabt),
    ]
    out_shape = [
        jax.ShapeDtypeStruct((n, RET_WIDTH), bf16), jax.ShapeDtypeStruct((n, RET_WIDTH), bf16),
        jax.ShapeDtypeStruct((n, RET_WIDTH), bf16), jax.ShapeDtypeStruct((n, ATT_WIDTH), bf16),
        jax.ShapeDtypeStruct((n, ATT_WIDTH), f32), jax.ShapeDtypeStruct((n, ATT_WIDTH), f32),
        jax.ShapeDtypeStruct((IDX_HEADS * IDX_DIM, n), bf16),
        jax.ShapeDtypeStruct((n, IDX_DIM), f32),
        jax.ShapeDtypeStruct((IDX_HEADS, n), f32),
    ]
    out_specs = [
        pl.BlockSpec((tm, RET_WIDTH), row), pl.BlockSpec((tm, RET_WIDTH), row),
        pl.BlockSpec((tm, RET_WIDTH), row), pl.BlockSpec((tm, ATT_WIDTH), row),
        pl.BlockSpec((tm, ATT_WIDTH), row), pl.BlockSpec((tm, ATT_WIDTH), row),
        pl.BlockSpec((IDX_HEADS * IDX_DIM, tm), col),
        pl.BlockSpec((tm, IDX_DIM), row),
        pl.BlockSpec((IDX_HEADS, tm), col),
    ]
    return pl.pallas_call(
        _inproj_kernel, grid=(nj, nr), in_specs=in_specs, out_specs=out_specs, out_shape=out_shape,
        compiler_params=_cparams(("arbitrary", "arbitrary")), name="inproj",
    )(x2d, w["ng"], w["wr"], w["wa"], w["wiq"], w["wik"], w["qg"], w["kg"], w["ig"],
      rc, rs, ac, as_, ic, is_, ict, ist)


def _retention_kernel(*refs, has_state):
    if has_state:
        rq_ref, rk_ref, rv_ref, dm_ref, qd_ref, kd_ref, cd_ref, g_ref, s0_ref, o_ref, st_ref = refs
    else:
        rq_ref, rk_ref, rv_ref, dm_ref, qd_ref, kd_ref, cd_ref, g_ref, o_ref, st_ref = refs

    @pl.when(pl.program_id(1) == 0)
    def _():
        if has_state:
            st_ref[...] = s0_ref[...]
        else:
            st_ref[...] = jnp.zeros(st_ref.shape, f32)

    for hd in range(RET_HEADS):
        sl = slice(hd * RET_DK, (hd + 1) * RET_DK)
        q, k, v = rq_ref[:, sl], rk_ref[:, sl], rv_ref[:, sl]
        st = st_ref[hd]
        s = lax.dot_general(q, k, _NT, preferred_element_type=f32) * dm_ref[hd]
        inner = jnp.dot(s.astype(bf16), v, preferred_element_type=f32)
        qd = (q.astype(f32) * qd_ref[hd]).astype(bf16)
        cross = jnp.dot(qd, st.astype(bf16), preferred_element_type=f32)
        kdt = (k.astype(f32) * kd_ref[hd]).T.astype(bf16)
        st_ref[hd] = cd_ref[hd] * st + jnp.dot(kdt, v, preferred_element_type=f32)
        o = inner + cross
        o = (o * lax.rsqrt(jnp.mean(o * o, axis=-1, keepdims=True) + EPS)) * g_ref[hd]
        o_ref[:, sl] = o


def _ret_log_decay():
    return jnp.log1p(-jnp.exp2(-5.0 - jnp.arange(RET_HEADS, dtype=f32)))


def _retention_tables(c_block, c_true):
    log_g = _ret_log_decay()
    i = jnp.arange(c_block, dtype=f32)
    diff = i[:, None] - i[None, :]
    dmask = jnp.where(diff >= 0, jnp.exp(log_g[:, None, None] * jnp.maximum(diff, 0.0)), 0.0)
    q_dec = jnp.exp(log_g[:, None] * (i + 1.0)[None, :])
    k_dec = jnp.where(i[None, :] < c_true, jnp.exp(log_g[:, None] * (c_true - 1.0 - i)[None, :]), 0.0)
    chunk_decay = jnp.exp(log_g * c_true)
    bc = lambda a: jnp.broadcast_to(a[:, :, None], (RET_HEADS, c_block, LANES))
    cd = jnp.broadcast_to(chunk_decay[:, None, None], (RET_HEADS, 1, LANES))
    return dmask, bc(q_dec), bc(k_dec), cd


def _retention(rq, rk, rv, ret_g, n_batch, c_true, state=None, layer=0):
    n = rq.shape[0]
    c = RET_CHUNK
    nc = n // n_batch // c
    dm, qd, kd, cd = _retention_tables(c, c_true)
    g = ret_g.reshape(RET_HEADS, 1, RET_DV)
    row = lambda b, j: (b * nc + j, 0)
    c3 = lambda b, j: (0, 0, 0)
    in_specs = [
        pl.BlockSpec((c, RET_WIDTH), row), pl.BlockSpec((c, RET_WIDTH), row), pl.BlockSpec((c, RET_WIDTH), row),
        pl.BlockSpec(dm.shape, c3), pl.BlockSpec(qd.shape, c3), pl.BlockSpec(kd.shape, c3),
        pl.BlockSpec(cd.shape, c3), pl.BlockSpec(g.shape, c3),
    ]
    args = [rq, rk, rv, dm, qd, kd, cd, g]
    if state is not None:
        in_specs.append(pl.BlockSpec((None, None, RET_HEADS, RET_DK, RET_DV), lambda b, j: (layer, b, 0, 0, 0)))
        args.append(state)
    return pl.pallas_call(
        functools.partial(_retention_kernel, has_state=state is not None),
        grid=(n_batch, nc), in_specs=in_specs,
        out_specs=[pl.BlockSpec((c, RET_WIDTH), row),
                   pl.BlockSpec((None, RET_HEADS, RET_DK, RET_DV), lambda b, j: (b, 0, 0, 0))],
        out_shape=[jax.ShapeDtypeStruct((n, RET_WIDTH), f32),
                   jax.ShapeDtypeStruct((n_batch, RET_HEADS, RET_DK, RET_DV), f32)],
        compiler_params=_cparams(("arbitrary", "arbitrary")), name="retention",
    )(*args)


def _radix_select(count_ge, count_eq_before, k_sel, shape, idx_bits):
    def value_pass(i, t):
        cand = t + lax.shift_left(jnp.int32(1), 31 - i)
        return jnp.where(count_ge(cand) >= k_sel, cand, t)

    t = lax.fori_loop(0, 32, value_pass, jnp.full(shape, INT_MIN, i32))
    need = k_sel - count_ge(t + 1)

    def index_pass(i, c):
        cand = c + lax.shift_left(jnp.int32(1), idx_bits - 1 - i)
        return jnp.where(count_eq_before(t, cand) < need, cand, c)

    c = lax.fori_loop(0, idx_bits, index_pass, jnp.zeros(shape, i32))
    return t, c


def _dsa_prompt_kernel(iqt_ref, iwt_ref, ik_ref, aq_ref, ak_ref, av_ref, o_ref, key_ref, bias_ref, *, k_sel, idx_bits):
    qb = pl.program_id(1)
    nk = qb + 1
    blk = Q_BLOCK
    row_i = lax.broadcasted_iota(i32, (blk, blk), 0)
    col_i = lax.broadcasted_iota(i32, (blk, blk), 1)

    def score_blk(kb, carry):
        ikb = ik_ref[pl.ds(pl.multiple_of(kb * blk, blk), blk), :].astype(bf16)
        acc = jnp.zeros((blk, blk), f32)
        for hd in range(IDX_HEADS):
            s = jnp.dot(ikb, iqt_ref[hd * IDX_DIM:(hd + 1) * IDX_DIM, :], preferred_element_type=f32)
            acc = acc + jnp.maximum(s * (IDX_DIM ** -0.5), 0.0) * iwt_ref[hd:hd + 1, :]
        causal = (row_i + kb * blk) <= (col_i + qb * blk)
        key_ref[kb] = _sortable(jnp.where(causal, acc, -jnp.inf))
        return carry

    lax.fori_loop(0, nk, score_blk, 0)

    def reduce_rows(m):
        part = jnp.sum(m.astype(i32).reshape(blk // SUBLANES, SUBLANES, blk), axis=0)
        return part

    def count_ge(cand):
        def body(kb, c):
            return c + reduce_rows(key_ref[kb] >= cand)
        c8 = lax.fori_loop(0, nk, body, jnp.zeros((SUBLANES, blk), i32))
        return jnp.sum(c8, axis=0, keepdims=True)

    def count_eq_before(t, cand):
        def body(kb, c):
            return c + reduce_rows((key_ref[kb] == t) & ((row_i + kb * blk) < cand))
        c8 = lax.fori_loop(0, nk, body, jnp.zeros((SUBLANES, blk), i32))
        return jnp.sum(c8, axis=0, keepdims=True)

    t, c = _radix_select(count_ge, count_eq_before, k_sel, (1, blk), idx_bits)

    def bias_blk(kb, carry):
        kk = key_ref[kb]
        sel = (kk > t) | ((kk == t) & ((row_i + kb * blk) <= c))
        sel = sel & (kk > _NEG_INF_KEY)
        bias_ref[kb] = jnp.where(sel, 0.0, MASK_NEG).T
        return carry

    lax.fori_loop(0, nk, bias_blk, 0)

    def attn_blk(kb, carry):
        ms, ls, accs = carry
        off = pl.multiple_of(kb * blk, blk)
        bias = bias_ref[kb]
        new_m, new_l, new_acc = [], [], []
        for hd in range(ATT_HEADS):
            sl = slice(hd * HEAD_DIM, (hd + 1) * HEAD_DIM)
            kh = ak_ref[pl.ds(off, blk), sl].astype(bf16)
            vh = av_ref[pl.ds(off, blk), sl].astype(bf16)
            s = lax.dot_general(aq_ref[:, sl], kh, _NT, preferred_element_type=f32) * (HEAD_DIM ** -0.5) + bias
            m_new = jnp.maximum(ms[hd], jnp.max(s, axis=-1, keepdims=True))
            alpha = jnp.exp(ms[hd] - m_new)
            p = jnp.exp(s - m_new)
            new_l.append(alpha * ls[hd] + jnp.sum(p, axis=-1, keepdims=True))
            new_acc.append(alpha * accs[hd] + jnp.dot(p.astype(bf16), vh, preferred_element_type=f32))
            new_m.append(m_new)
        return tuple(new_m), tuple(new_l), tuple(new_acc)

    init = (tuple(jnp.full((blk, 1), MASK_NEG, f32) for _ in range(ATT_HEADS)),
            tuple(jnp.zeros((blk, 1), f32) for _ in range(ATT_HEADS)),
            tuple(jnp.zeros((blk, HEAD_DIM), f32) for _ in range(ATT_HEADS)))
    ms, ls, accs = lax.fori_loop(0, nk, attn_blk, init)
    for hd in range(ATT_HEADS):
        o_ref[:, hd * HEAD_DIM:(hd + 1) * HEAD_DIM] = accs[hd] / ls[hd]


def _dsa_prompt(iqt, iwt, ik, aq, ak, av, n_batch):
    n = aq.shape[0]
    t = n // n_batch
    nqb = t // Q_BLOCK
    k_sel = min(TOPK_MAX, t // 4)
    idx_bits = max(1, (t - 1).bit_length())
    qrow = lambda b, q: (b * nqb + q, 0)
    qcol = lambda b, q: (0, b * nqb + q)
    brow = lambda b, q: (b, 0)
    return pl.pallas_call(
        functools.partial(_dsa_prompt_kernel, k_sel=k_sel, idx_bits=idx_bits),
        grid=(n_batch, nqb),
        in_specs=[
            pl.BlockSpec((IDX_HEADS * IDX_DIM, Q_BLOCK), qcol),
            pl.BlockSpec((IDX_HEADS, Q_BLOCK), qcol),
            pl.BlockSpec((t, IDX_DIM), brow),
            pl.BlockSpec((Q_BLOCK, ATT_WIDTH), qrow),
            pl.BlockSpec((t, ATT_WIDTH), brow),
            pl.BlockSpec((t, ATT_WIDTH), brow),
        ],
        out_specs=pl.BlockSpec((Q_BLOCK, ATT_WIDTH), qrow),
        out_shape=jax.ShapeDtypeStruct((n, ATT_WIDTH), f32),
        scratch_shapes=[pltpu.VMEM((nqb, Q_BLOCK, Q_BLOCK), i32), pltpu.VMEM((nqb, Q_BLOCK, Q_BLOCK), f32)],
        compiler_params=_cparams(("arbitrary", "arbitrary")), name="dsa_prompt",
    )(iqt, iwt, ik, aq, ak, av)


def _page_scores(iq, iw, ik_page):
    s = lax.dot_general(iq, ik_page.astype(bf16), _NT, preferred_element_type=f32)
    s = jnp.maximum(s * (IDX_DIM ** -0.5), 0.0) * iw
    return jnp.sum(s.reshape(IDX_HEADS, SAMPLE_Q, PAGE_SIZE), axis=0)


def _idx_scores_kernel(pt_ref, iq_ref, iw_ref, *refs, pages):
    page_refs, o_ref = refs[:pages], refs[pages]
    iq, iw = iq_ref[...], iw_ref[...]
    for i in range(pages):
        o_ref[i] = _page_scores(iq, iw, page_refs[i][...])


def _idx_scores(page_table, iq_s, iw_s, cache_kidx, layer, pages):
    db, n_pages = page_table.shape
    nj = n_pages // pages
    page_spec = lambda i: pl.BlockSpec((None, None, PAGE_SIZE, IDX_DIM),
                                       lambda b, j, pt: (layer, pt[b, j * pages + i], 0, 0))
    return pl.pallas_call(
        functools.partial(_idx_scores_kernel, pages=pages),
        grid_spec=pltpu.PrefetchScalarGridSpec(
            num_scalar_prefetch=1, grid=(db, nj),
            in_specs=[pl.BlockSpec((None, IDX_HEADS * SAMPLE_Q, IDX_DIM), lambda b, j, pt: (b, 0, 0)),
                      pl.BlockSpec((None, IDX_HEADS * SAMPLE_Q, LANES), lambda b, j, pt: (b, 0, 0))]
                     + [page_spec(i) for i in range(pages)],
            out_specs=pl.BlockSpec((None, pages, SAMPLE_Q, PAGE_SIZE), lambda b, j, pt: (b, j, 0, 0)),
        ),
        out_shape=jax.ShapeDtypeStruct((db, n_pages, SAMPLE_Q, PAGE_SIZE), f32),
        compiler_params=_cparams(("arbitrary", "arbitrary")), name="idx_scores",
    )(page_table, iq_s, iw_s, *([cache_kidx] * pages))


def _select_kernel(sc_ref, iq_ref, iw_ref, ikn_ref, bias_ref, key_ref, *, k_sel, idx_bits):
    n_req, n_pages = sc_ref.shape[0], sc_ref.shape[1]
    rows = n_req * SAMPLE_Q
    nch = n_pages + 1
    lane_i = lax.broadcasted_iota(i32, (rows, LANES), 1)
    q_i = lax.broadcasted_iota(i32, (SAMPLE_Q, LANES), 0)
    l_i = lax.broadcasted_iota(i32, (SAMPLE_Q, LANES), 1)

    def to_key(ch, carry):
        key_ref[ch] = _sortable(jnp.concatenate([sc_ref[r, ch] for r in range(n_req)], axis=0))
        return carry

    lax.fori_loop(0, n_pages, to_key, 0)
    new = []
    for r in range(n_req):
        s = _page_scores(iq_ref[r], iw_ref[r], ikn_ref[r])
        new.append(jnp.where(l_i <= q_i, s, -jnp.inf))
    key_ref[n_pages] = _sortable(jnp.concatenate(new, axis=0))

    def count_ge(cand):
        cand_b = jnp.broadcast_to(cand, (rows, LANES))

        def body(ch, c):
            return c + (key_ref[ch] >= cand_b).astype(i32)
        c = lax.fori_loop(0, nch, body, jnp.zeros((rows, LANES), i32))
        return jnp.sum(c, axis=-1, keepdims=True)

    def count_eq_before(t, cand):
        t_b = jnp.broadcast_to(t, (rows, LANES))
        cand_b = jnp.broadcast_to(cand, (rows, LANES))

        def body(ch, c):
            return c + ((key_ref[ch] == t_b) & ((lane_i + ch * LANES) < cand_b)).astype(i32)
        c = lax.fori_loop(0, nch, body, jnp.zeros((rows, LANES), i32))
        return jnp.sum(c, axis=-1, keepdims=True)

    t, c = _radix_select(count_ge, count_eq_before, k_sel, (rows, 1), idx_bits)
    t_b = jnp.broadcast_to(t, (rows, LANES))
    c_b = jnp.broadcast_to(c, (rows, LANES))

    def to_bias(ch, carry):
        kk = key_ref[ch]
        sel = (kk > t_b) | ((kk == t_b) & ((lane_i + ch * LANES) <= c_b))
        sel = sel & (kk > _NEG_INF_KEY)
        bias = jnp.where(sel, 0.0, MASK_NEG)
        for r in range(n_req):
            bias_ref[r, ch] = bias[r * SAMPLE_Q:(r + 1) * SAMPLE_Q]
        return carry

    lax.fori_loop(0, nch, to_bias, 0)


def _select(scores, iq_s, iw_s, ik_new, k_sel, n_req):
    db, n_pages = scores.shape[0], scores.shape[1]
    nch = n_pages + 1
    idx_bits = max(1, (nch * LANES - 1).bit_length())
    req3 = lambda g: (g, 0, 0)
    req4 = lambda g: (g, 0, 0, 0)
    return pl.pallas_call(
        functools.partial(_select_kernel, k_sel=k_sel, idx_bits=idx_bits),
        grid=(db // n_req,),
        in_specs=[pl.BlockSpec((n_req, n_pages, SAMPLE_Q, LANES), req4),
                  pl.BlockSpec((n_req, IDX_HEADS * SAMPLE_Q, IDX_DIM), req3),
                  pl.BlockSpec((n_req, IDX_HEADS * SAMPLE_Q, LANES), req3),
                  pl.BlockSpec((n_req, PAGE_SIZE, IDX_DIM), req3)],
        out_specs=pl.BlockSpec((n_req, nch, SAMPLE_Q, LANES), req4),
        out_shape=jax.ShapeDtypeStruct((db, nch, SAMPLE_Q, LANES), f32),
        scratch_shapes=[pltpu.VMEM((nch, n_req * SAMPLE_Q, LANES), i32)],
        compiler_params=_cparams(("arbitrary",)), name="select",
    )(scores, iq_s, iw_s, ik_new)


def _paged_attn_kernel(pt_ref, q_ref, bias_ref, biasn_ref, kn_ref, vn_ref, *refs, pages):
    k_refs, v_refs = refs[:pages], refs[pages:2 * pages]
    o_ref, m_ref, l_ref, acc_ref = refs[2 * pages:]
    j = pl.program_id(1)

    @pl.when(j == 0)
    def _():
        m_ref[...] = jnp.full(m_ref.shape, MASK_NEG, f32)
        l_ref[...] = jnp.zeros(l_ref.shape, f32)
        acc_ref[...] = jnp.zeros(acc_ref.shape, f32)

    def update(hd, kh, vh, bias):
        sl = slice(hd * HEAD_DIM, (hd + 1) * HEAD_DIM)
        s = lax.dot_general(q_ref[:, sl], kh.astype(bf16), _NT, preferred_element_type=f32) * (HEAD_DIM ** -0.5) + bias
        m_old = m_ref[hd]
        m_new = jnp.maximum(m_old, jnp.max(s, axis=-1, keepdims=True))
        alpha = jnp.exp(m_old - m_new)
        p = jnp.exp(s - m_new)
        l_ref[hd] = alpha * l_ref[hd] + jnp.sum(p, axis=-1, keepdims=True)
        acc_ref[hd] = alpha * acc_ref[hd] + jnp.dot(p.astype(bf16), vh.astype(bf16), preferred_element_type=f32)
        m_ref[hd] = m_new

    for i in range(pages):
        b8 = bias_ref[i]
        bias = jnp.concatenate([b8, b8], axis=0)
        for hd in range(ATT_HEADS):
            update(hd, k_refs[i][:, hd, :], v_refs[i][:, hd, :], bias)

    @pl.when(j == pl.num_programs(1) - 1)
    def _():
        b8 = biasn_ref[...]
        bias = jnp.concatenate([b8, b8], axis=0)
        for hd in range(ATT_HEADS):
            sl = slice(hd * HEAD_DIM, (hd + 1) * HEAD_DIM)
            update(hd, kn_ref[:, sl], vn_ref[:, sl], bias)
            o_ref[:, sl] = acc_ref[hd] / l_ref[hd]


def _paged_attn(page_table, aq_s, bias, k_new, v_new, cache_k, cache_v, layer, pages):
    db, n_pages = page_table.shape
    nj = n_pages // pages
    page_spec = lambda i: pl.BlockSpec((None, None, PAGE_SIZE, ATT_HEADS, HEAD_DIM),
                                       lambda b, j, pt: (layer, pt[b, j * pages + i], 0, 0, 0))
    req = lambda b, j, pt: (b, 0)
    return pl.pallas_call(
        functools.partial(_paged_attn_kernel, pages=pages),
        grid_spec=pltpu.PrefetchScalarGridSpec(
            num_scalar_prefetch=1, grid=(db, nj),
            in_specs=[pl.BlockSpec((SAMPLE_ROWS, ATT_WIDTH), req),
                      pl.BlockSpec((None, pages, SAMPLE_Q, LANES), lambda b, j, pt: (b, j, 0, 0)),
                      pl.BlockSpec((None, None, SAMPLE_Q, LANES), lambda b, j, pt: (b, n_pages, 0, 0)),
                      pl.BlockSpec((PAGE_SIZE, ATT_WIDTH), req),
                      pl.BlockSpec((PAGE_SIZE, ATT_WIDTH), req)]
                     + [page_spec(i) for i in range(pages)] * 2,
            out_specs=pl.BlockSpec((SAMPLE_ROWS, ATT_WIDTH), req),
            scratch_shapes=[pltpu.VMEM((ATT_HEADS, SAMPLE_ROWS, 1), f32),
                            pltpu.VMEM((ATT_HEADS, SAMPLE_ROWS, 1), f32),
                            pltpu.VMEM((ATT_HEADS, SAMPLE_ROWS, HEAD_DIM), f32)],
        ),
        out_shape=jax.ShapeDtypeStruct((db * SAMPLE_ROWS, ATT_WIDTH), f32),
        compiler_params=_cparams(("arbitrary", "arbitrary")), name="paged_attn",
    )(page_table, aq_s, bias, bias, k_new, v_new, *([cache_k] * pages), *([cache_v] * pages))


def _merge_kernel(x_ref, oret_ref, oatt_ref, ng_ref, wg_ref, wbr_ref, wba_ref, wo_ref, y_ref):
    x = x_ref[...]
    ms = jnp.mean(x * x, axis=-1, keepdims=True)
    h = ((x * lax.rsqrt(ms + EPS)) * ng_ref[...]).astype(bf16)
    g = jnp.dot(h, wg_ref[...], preferred_element_type=f32)
    rg = g[:, 0:RET_WIDTH]
    ag = g[:, RET_WIDTH:RET_WIDTH + ATT_WIDTH]
    ga = g[:, RET_WIDTH + ATT_WIDTH:RET_WIDTH + ATT_WIDTH + D_MODEL]
    gb = g[:, RET_WIDTH + ATT_WIDTH + D_MODEL:]
    o_ret = oret_ref[...] * (rg * jax.nn.sigmoid(rg))
    o_att = oatt_ref[...] * (ag * jax.nn.sigmoid(ag))
    a = jnp.dot(o_ret.astype(bf16), wbr_ref[...], preferred_element_type=f32)
    b = jnp.dot(o_att.astype(bf16), wba_ref[...], preferred_element_type=f32)
    y = jax.nn.sigmoid(ga) * a + jax.nn.sigmoid(gb) * b
    y_ref[...] = x + jnp.dot(y.astype(bf16), wo_ref[...], preferred_element_type=f32)


def _merge(x2d, o_ret, o_att, w, tm):
    n = x2d.shape[0]
    row = lambda i: (i, 0)
    full = lambda a: pl.BlockSpec(a.shape, lambda i: (0, 0))
    return pl.pallas_call(
        _merge_kernel, grid=(n // tm,),
        in_specs=[pl.BlockSpec((tm, D_MODEL), row), pl.BlockSpec((tm, RET_WIDTH), row),
                  pl.BlockSpec((tm, ATT_WIDTH), row),
                  full(w["ng"]), full(w["wg"]), full(w["wbr"]), full(w["wba"]), full(w["wo"])],
        out_specs=pl.BlockSpec((tm, D_MODEL), row),
        out_shape=jax.ShapeDtypeStruct((n, D_MODEL), f32),
        compiler_params=_cparams(("arbitrary",)), name="merge",
    )(x2d, o_ret, o_att, w["ng"], w["wg"], w["wbr"], w["wba"], w["wo"])


def _prep_weights(l, norm_g, w_in, q_norm_g, k_norm_g, kidx_norm_g, w_br_ret, w_br_att, w_out):
    w = w_in[l]
    pad_cols = lambda a, n: jnp.pad(a, ((0, 0), (0, n - a.shape[1])))
    wiq = jnp.concatenate([w[:, _O_IQ:_O_IK], w[:, _O_IW:_O_GA]], axis=1).T
    wiq = jnp.pad(wiq, ((0, 2 * SUBLANES - IDX_HEADS), (0, 0)))
    return {
        "ng": norm_g[l].reshape(1, D_MODEL),
        "wr": w[:, _O_RQ:_O_RG].astype(bf16),
        "wa": jnp.concatenate([w[:, _O_AQ:_O_AG]], axis=1).astype(bf16),
        "wiq": wiq.astype(bf16),
        "wik": pad_cols(w[:, _O_IK:_O_IW], LANES).astype(bf16),
        "qg": q_norm_g[l].reshape(1, HEAD_DIM),
        "kg": k_norm_g[l].reshape(1, HEAD_DIM),
        "ig": pad_cols(kidx_norm_g[l].reshape(1, IDX_DIM), LANES),
        "wg": jnp.concatenate([w[:, _O_RG:_O_AQ], w[:, _O_AG:_O_IQ], w[:, _O_GA:_O_END]], axis=1).astype(bf16),
        "wbr": w_br_ret[l].astype(bf16),
        "wba": w_br_att[l].astype(bf16),
        "wo": w_out[l].astype(bf16),
    }


def _position_tables(pos):
    rc, rs, _, _ = _rope_tables(pos, RET_DK, RET_ROPE_THETA, LANES)
    ac, as_, _, _ = _rope_tables(pos, ROPE_DIM, ROPE_THETA, LANES)
    ic, is_, ict, ist = _rope_tables(pos, IDX_ROPE_DIM, ROPE_THETA, LANES)
    return rc, rs, ac, as_, ic, is_, ict, ist


def _token_tile(n):
    for tm in (256, 128):
        if n % tm == 0:
            return tm
    assert n % SAMPLE_ROWS == 0 and n < LANES, n
    return n


def _prompt_layer(x2d, n_batch, w, ret_g, tabs, tm):
    rq, rk, rv, aq, ak, av, iqt, ik, iwt = _inproj(x2d, w, tabs, tm)
    o_ret, s_new = _retention(rq, rk, rv, ret_g, n_batch, RET_CHUNK)
    o_att = _dsa_prompt(iqt, iwt, ik, aq, ak, av, n_batch)
    y = _merge(x2d, o_ret, o_att, w, tm)
    return y, ak, av, ik, s_new


def _sample_layer(xs, n_true, w, ret_g, tabs, tm, layer, state_ret, page_table, cache_k, cache_v, cache_kidx, pages):
    db = xs.shape[0]
    n = db * SAMPLE_ROWS
    past = page_table.shape[1] * PAGE_SIZE
    x2d = xs.reshape(n, D_MODEL)
    rq, rk, rv, aq, ak, av, iqt, ik, iwt = _inproj(x2d, w, tabs, tm)

    padc = lambda a: jnp.pad(a.reshape(db, SAMPLE_ROWS, -1), ((0, 0), (0, RET_CHUNK - SAMPLE_ROWS), (0, 0))
                             ).reshape(db * RET_CHUNK, -1)
    o_ret, s_new = _retention(padc(rq), padc(rk), padc(rv), ret_g, db, n_true, state=state_ret, layer=layer)
    o_ret = o_ret.reshape(db, RET_CHUNK, RET_WIDTH)[:, :SAMPLE_ROWS].reshape(n, RET_WIDTH)

    iq_s = iqt.reshape(IDX_HEADS, IDX_DIM, db, SAMPLE_ROWS)[:, :, :, :SAMPLE_Q]
    iq_s = iq_s.transpose(2, 0, 3, 1).reshape(db, IDX_HEADS * SAMPLE_Q, IDX_DIM)
    iw_s = iwt.reshape(IDX_HEADS, db, SAMPLE_ROWS)[:, :, :SAMPLE_Q].transpose(1, 0, 2)
    iw_s = jnp.broadcast_to(iw_s.reshape(db, IDX_HEADS * SAMPLE_Q, 1), (db, IDX_HEADS * SAMPLE_Q, LANES))
    ik_new = jnp.pad(ik.reshape(db, SAMPLE_ROWS, IDX_DIM), ((0, 0), (0, PAGE_SIZE - SAMPLE_ROWS), (0, 0)))

    scores = _idx_scores(page_table, iq_s, iw_s, cache_kidx, layer, pages)
    k_sel = min(TOPK_MAX, (past + n_true) // 4)
    n_req = 8 if db % 8 == 0 else db
    bias = _select(scores, iq_s, iw_s, ik_new, k_sel, n_req)

    pad_new = lambda a: jnp.pad(a.reshape(db, SAMPLE_ROWS, ATT_WIDTH), ((0, 0), (0, PAGE_SIZE - SAMPLE_ROWS), (0, 0))
                                ).reshape(db * PAGE_SIZE, ATT_WIDTH)
    o_att = _paged_attn(page_table, aq, bias, pad_new(ak), pad_new(av), cache_k, cache_v, layer, pages)
    y = _merge(x2d, o_ret, o_att, w, tm)
    return y.reshape(db, SAMPLE_ROWS, D_MODEL), ak, av, ik, s_new


def kernel(x_prompt, x_sample, cache_k, cache_v, cache_kidx, state_ret, page_table, norm_g, w_in, q_norm_g,
           k_norm_g, kidx_norm_g, ret_norm_g, w_br_ret, w_br_att, w_out):
    depth = w_in.shape[0]
    n_batch, t, _ = x_prompt.shape
    db, ds, _ = x_sample.shape
    n_pages = page_table.shape[1]
    past = n_pages * PAGE_SIZE
    assert t % RET_CHUNK == 0 and ds <= SAMPLE_Q

    tm_p = _token_tile(t)
    tabs_p = _position_tables(jnp.arange(t))
    n_s = db * SAMPLE_ROWS
    tm_s = _token_tile(n_s)
    pos_s = past + (jnp.arange(tm_s) % SAMPLE_ROWS)
    tabs_s = _position_tables(pos_s)
    pages = 16 if n_pages % 16 == 0 else n_pages

    xp = x_prompt.reshape(n_batch * t, D_MODEL)
    pad_s = lambda a: jnp.pad(a, ((0, 0), (0, SAMPLE_ROWS - ds), (0, 0)))
    xs = pad_s(x_sample)
    kp, vp, ip, sp, kd, vd, idd, sd = [], [], [], [], [], [], [], []
    for l in range(depth):
        w = _prep_weights(l, norm_g, w_in, q_norm_g, k_norm_g, kidx_norm_g, w_br_ret, w_br_att, w_out)
        xp, k_new, v_new, i_new, s_new = _prompt_layer(xp, n_batch, w, ret_norm_g[l], tabs_p, tm_p)
        kp.append(k_new.reshape(n_batch, t, ATT_HEADS, HEAD_DIM))
        vp.append(v_new.reshape(n_batch, t, ATT_HEADS, HEAD_DIM))
        ip.append(i_new.reshape(n_batch, t, IDX_DIM))
        sp.append(s_new)
        ys, k_new, v_new, i_new, s_new = _sample_layer(
            xs, ds, w, ret_norm_g[l], tabs_s, tm_s, l, state_ret, page_table, cache_k, cache_v, cache_kidx, pages)
        xs = pad_s(ys[:, :ds])
        kd.append(k_new.reshape(db, SAMPLE_ROWS, ATT_HEADS, HEAD_DIM)[:, :ds])
        vd.append(v_new.reshape(db, SAMPLE_ROWS, ATT_HEADS, HEAD_DIM)[:, :ds])
        idd.append(i_new.reshape(db, SAMPLE_ROWS, IDX_DIM)[:, :ds])
        sd.append(s_new)
    return (xp.reshape(n_batch, t, D_MODEL), xs[:, :ds],
            jnp.stack(kp), jnp.stack(vp), jnp.stack(ip), jnp.stack(sp),
            jnp.stack(kd), jnp.stack(vd), jnp.stack(idd), jnp.stack(sd))
```

```python
import functools
import math

import jax
import jax.numpy as jnp
from jax import lax
from jax.experimental import pallas as pl
from jax.experimental.pallas import tpu as pltpu

D_MODEL = 1024
PAGE_SIZE = 128
RET_HEADS = 4
RET_DK = 128
RET_DV = 128
RET_WIDTH = RET_HEADS * RET_DV
RET_CHUNK = 128
RET_ROPE_THETA = 10000.0
ATT_HEADS = 4
HEAD_DIM = 128
ATT_WIDTH = ATT_HEADS * HEAD_DIM
ROPE_DIM = HEAD_DIM // 4
ROPE_THETA = 500000.0
IDX_HEADS = 8
IDX_DIM = 64
IDX_ROPE_DIM = IDX_DIM // 4
TOPK_MAX = 256
Q_BLOCK = 128
EPS = 1e-6

LANES = 128
SUBLANES = 8
VMEM_LIMIT = 56 * 1024 * 1024

_O_RQ, _O_RK, _O_RV, _O_RG = 0, 512, 1024, 1536
_O_AQ, _O_AK, _O_AV, _O_AG = 2048, 2560, 3072, 3584
_O_IQ, _O_IK, _O_IW = 4096, 4608, 4672
_O_GA, _O_GB, _O_END = 4680, 5704, 6728

KEY_CHUNK = 512
SAMPLE_ROWS = 16
SAMPLE_Q = 8
MASK_NEG = -1e30
INT_MIN = -(2 ** 31)

f32 = jnp.float32
bf16 = jnp.bfloat16
i32 = jnp.int32
i16 = jnp.int16

_NT = (((1,), (1,)), ((), ()))


def _cparams(sem, flags=None):
    return pltpu.CompilerParams(dimension_semantics=sem, vmem_limit_bytes=VMEM_LIMIT, flags=flags)


def _sortable(s):
    b = pltpu.bitcast(s + 0.0, i32)
    return b ^ ((b >> 31) & jnp.int32(0x7FFFFFFF))


_NEG_INF_KEY = -2139095041


def _rope_tables(pos, rot_dim, theta, width):
    half = rot_dim // 2
    freqs = jnp.exp(-jnp.arange(half, dtype=f32) * (2.0 / rot_dim) * math.log(theta))
    ang = pos.astype(f32)[:, None] * freqs[None, :]
    cos, sin = jnp.cos(ang), jnp.sin(ang)
    t = pos.shape[0]
    c = jnp.concatenate([cos, cos, jnp.ones((t, width - rot_dim), f32)], axis=-1)
    s = jnp.concatenate([sin, -sin, jnp.zeros((t, width - rot_dim), f32)], axis=-1)
    return c, s, cos.T, sin.T


def _rope_lanes(x, c, s, half, rot, lane):
    w = x * s
    if rot == LANES:
        return x * c + pltpu.roll(w, half, 1)
    up = pltpu.roll(w, LANES - half, 1)
    dn = pltpu.roll(w, half, 1)
    return x * c + jnp.where(lane < half, up, jnp.where(lane < rot, dn, 0.0))


def _inproj_kernel(x_ref, ng_ref, wr_ref, wa_ref, wiq_ref, wik_ref, qg_ref, kg_ref, ig_ref,
                   rc_ref, rs_ref, ac_ref, as_ref, ic_ref, is_ref, ict_ref, ist_ref,
                   rq_ref, rk_ref, rv_ref, aq_ref, ak_ref, av_ref, akb_ref, avb_ref, iqt_ref, ik_ref, iwt_ref):
    x = x_ref[...]
    tm = x.shape[0]
    ms = jnp.mean(x * x, axis=-1, keepdims=True)
    h = ((x * lax.rsqrt(ms + EPS)) * ng_ref[...]).astype(bf16)
    lane = lax.broadcasted_iota(i32, (tm, LANES), 1)

    r = jnp.dot(h, wr_ref[...], preferred_element_type=f32)
    rc, rs = rc_ref[...], rs_ref[...]
    for hd in range(RET_HEADS):
        sl = slice(hd * RET_DK, (hd + 1) * RET_DK)
        q = r[:, _O_RQ + hd * RET_DK:_O_RQ + (hd + 1) * RET_DK]
        k = r[:, _O_RK + hd * RET_DK:_O_RK + (hd + 1) * RET_DK]
        rq_ref[:, sl] = _rope_lanes(q, rc, rs, RET_DK // 2, RET_DK, lane).astype(bf16)
        rk_ref[:, sl] = (_rope_lanes(k, rc, rs, RET_DK // 2, RET_DK, lane) * (RET_DK ** -0.5)).astype(bf16)
    rv_ref[...] = r[:, _O_RV:_O_RV + RET_WIDTH].astype(bf16)

    a = jnp.dot(h, wa_ref[...], preferred_element_type=f32)
    ac, as_ = ac_ref[...], as_ref[...]
    for hd in range(ATT_HEADS):
        sl = slice(hd * HEAD_DIM, (hd + 1) * HEAD_DIM)
        q = a[:, hd * HEAD_DIM:(hd + 1) * HEAD_DIM]
        k = a[:, ATT_WIDTH + hd * HEAD_DIM:ATT_WIDTH + (hd + 1) * HEAD_DIM]
        q = (q * lax.rsqrt(jnp.mean(q * q, axis=-1, keepdims=True) + EPS)) * qg_ref[...]
        k = (k * lax.rsqrt(jnp.mean(k * k, axis=-1, keepdims=True) + EPS)) * kg_ref[...]
        aq_ref[:, sl] = _rope_lanes(q, ac, as_, ROPE_DIM // 2, ROPE_DIM, lane).astype(bf16)
        k = _rope_lanes(k, ac, as_, ROPE_DIM // 2, ROPE_DIM, lane)
        ak_ref[:, hd, :] = k
        akb_ref[:, sl] = k.astype(bf16)
        av_ref[:, hd, :] = a[:, 2 * ATT_WIDTH + hd * HEAD_DIM:2 * ATT_WIDTH + (hd + 1) * HEAD_DIM]
    avb_ref[...] = a[:, 2 * ATT_WIDTH:3 * ATT_WIDTH].astype(bf16)

    pk = jnp.dot(h, wik_ref[...], preferred_element_type=f32)
    mk = jnp.sum(pk * pk, axis=-1, keepdims=True) * (1.0 / IDX_DIM)
    pk = (pk * lax.rsqrt(mk + EPS)) * ig_ref[...]
    pk = _rope_lanes(pk, ic_ref[...], is_ref[...], IDX_ROPE_DIM // 2, IDX_ROPE_DIM, lane)
    ik_ref[...] = pk[:, :IDX_DIM]

    t = lax.dot_general(wiq_ref[...], h, _NT, preferred_element_type=f32)
    ct, st = ict_ref[...], ist_ref[...]
    hr = IDX_ROPE_DIM // 2
    for hd in range(IDX_HEADS):
        base = hd * IDX_DIM
        x1 = t[base:base + hr]
        x2 = t[base + hr:base + 2 * hr]
        blk = jnp.concatenate([x1 * ct - x2 * st, x1 * st + x2 * ct, t[base + 2 * hr:base + IDX_DIM]], axis=0)
        iqt_ref[base:base + IDX_DIM, :] = blk.astype(bf16)
    iwt_ref[...] = t[IDX_HEADS * IDX_DIM:IDX_HEADS * IDX_DIM + IDX_HEADS] * (IDX_HEADS ** -0.5)


def _inproj(x2d, w, tabs, tm):
    n = x2d.shape[0]
    rc, rs, ac, as_, ic, is_, ict, ist = tabs
    tt = rc.shape[0]
    nj, nr = tt // tm, n // tt
    row = lambda j, r: (r * nj + j, 0)
    row3 = lambda j, r: (r * nj + j, 0, 0)
    col = lambda j, r: (0, r * nj + j)
    tab = lambda j, r: (j, 0)
    tabt = lambda j, r: (0, j)
    const = lambda j, r: (0, 0)
    full = lambda a: pl.BlockSpec(a.shape, const)
    in_specs = [
        pl.BlockSpec((tm, D_MODEL), row),
        full(w["ng"]), full(w["wr"]), full(w["wa"]), full(w["wiq"]), full(w["wik"]),
        full(w["qg"]), full(w["kg"]), full(w["ig"]),
        pl.BlockSpec((tm, LANES), tab), pl.BlockSpec((tm, LANES), tab),
        pl.BlockSpec((tm, LANES), tab), pl.BlockSpec((tm, LANES), tab),
        pl.BlockSpec((tm, LANES), tab), pl.BlockSpec((tm, LANES), tab),
        pl.BlockSpec((IDX_ROPE_DIM // 2, tm), tabt), pl.BlockSpec((IDX_ROPE_DIM // 2, tm), tabt),
    ]
    out_shape = [
        jax.ShapeDtypeStruct((n, RET_WIDTH), bf16), jax.ShapeDtypeStruct((n, RET_WIDTH), bf16),
        jax.ShapeDtypeStruct((n, RET_WIDTH), bf16), jax.ShapeDtypeStruct((n, ATT_WIDTH), bf16),
        jax.ShapeDtypeStruct((n, ATT_HEADS, HEAD_DIM), f32), jax.ShapeDtypeStruct((n, ATT_HEADS, HEAD_DIM), f32),
        jax.ShapeDtypeStruct((n, ATT_WIDTH), bf16), jax.ShapeDtypeStruct((n, ATT_WIDTH), bf16),
        jax.ShapeDtypeStruct((IDX_HEADS * IDX_DIM, n), bf16),
        jax.ShapeDtypeStruct((n, IDX_DIM), f32),
        jax.ShapeDtypeStruct((IDX_HEADS, n), f32),
    ]
    out_specs = [
        pl.BlockSpec((tm, RET_WIDTH), row), pl.BlockSpec((tm, RET_WIDTH), row),
        pl.BlockSpec((tm, RET_WIDTH), row), pl.BlockSpec((tm, ATT_WIDTH), row),
        pl.BlockSpec((tm, ATT_HEADS, HEAD_DIM), row3), pl.BlockSpec((tm, ATT_HEADS, HEAD_DIM), row3),
        pl.BlockSpec((tm, ATT_WIDTH), row), pl.BlockSpec((tm, ATT_WIDTH), row),
        pl.BlockSpec((IDX_HEADS * IDX_DIM, tm), col),
        pl.BlockSpec((tm, IDX_DIM), row),
        pl.BlockSpec((IDX_HEADS, tm), col),
    ]
    return pl.pallas_call(
        _inproj_kernel, grid=(nj, nr), in_specs=in_specs, out_specs=out_specs, out_shape=out_shape,
        compiler_params=_cparams(("arbitrary", "arbitrary")), name="inproj",
    )(x2d, w["ng"], w["wr"], w["wa"], w["wiq"], w["wik"], w["qg"], w["kg"], w["ig"],
      rc, rs, ac, as_, ic, is_, ict, ist)


def _retention_kernel(*refs, has_state):
    if has_state:
        rq_ref, rk_ref, rv_ref, dm_ref, qd_ref, kd_ref, cd_ref, g_ref, s0_ref, o_ref, st_ref = refs
    else:
        rq_ref, rk_ref, rv_ref, dm_ref, qd_ref, kd_ref, cd_ref, g_ref, o_ref, st_ref = refs

    @pl.when(pl.program_id(1) == 0)
    def _():
        if has_state:
            st_ref[...] = s0_ref[...]
        else:
            st_ref[...] = jnp.zeros(st_ref.shape, f32)

    for hd in range(RET_HEADS):
        sl = slice(hd * RET_DK, (hd + 1) * RET_DK)
        q, k, v = rq_ref[:, sl], rk_ref[:, sl], rv_ref[:, sl]
        st = st_ref[hd]
        s = lax.dot_general(q, k, _NT, preferred_element_type=f32) * dm_ref[hd]
        inner = jnp.dot(s.astype(bf16), v, preferred_element_type=f32)
        qd = (q.astype(f32) * qd_ref[hd]).astype(bf16)
        cross = jnp.dot(qd, st.astype(bf16), preferred_element_type=f32)
        kdt = (k.astype(f32) * kd_ref[hd]).T.astype(bf16)
        st_ref[hd] = cd_ref[hd] * st + jnp.dot(kdt, v, preferred_element_type=f32)
        o = inner + cross
        o = (o * lax.rsqrt(jnp.mean(o * o, axis=-1, keepdims=True) + EPS)) * g_ref[hd]
        o_ref[:, sl] = o


def _ret_log_decay():
    return jnp.log1p(-jnp.exp2(-5.0 - jnp.arange(RET_HEADS, dtype=f32)))


def _retention_tables(c_block, c_true):
    log_g = _ret_log_decay()
    i = jnp.arange(c_block, dtype=f32)
    diff = i[:, None] - i[None, :]
    dmask = jnp.where(diff >= 0, jnp.exp(log_g[:, None, None] * jnp.maximum(diff, 0.0)), 0.0)
    q_dec = jnp.exp(log_g[:, None] * (i + 1.0)[None, :])
    k_dec = jnp.where(i[None, :] < c_true, jnp.exp(log_g[:, None] * (c_true - 1.0 - i)[None, :]), 0.0)
    chunk_decay = jnp.exp(log_g * c_true)
    bc = lambda a: jnp.broadcast_to(a[:, :, None], (RET_HEADS, c_block, LANES))
    cd = jnp.broadcast_to(chunk_decay[:, None, None], (RET_HEADS, 1, LANES))
    return dmask, bc(q_dec), bc(k_dec), cd


def _retention(rq, rk, rv, ret_g, n_batch, c_true, state=None, layer=0):
    n = rq.shape[0]
    c = RET_CHUNK
    nc = n // n_batch // c
    dm, qd, kd, cd = _retention_tables(c, c_true)
    g = ret_g.reshape(RET_HEADS, 1, RET_DV)
    row = lambda b, j: (b * nc + j, 0)
    c3 = lambda b, j: (0, 0, 0)
    in_specs = [
        pl.BlockSpec((c, RET_WIDTH), row), pl.BlockSpec((c, RET_WIDTH), row), pl.BlockSpec((c, RET_WIDTH), row),
        pl.BlockSpec(dm.shape, c3), pl.BlockSpec(qd.shape, c3), pl.BlockSpec(kd.shape, c3),
        pl.BlockSpec(cd.shape, c3), pl.BlockSpec(g.shape, c3),
    ]
    args = [rq, rk, rv, dm, qd, kd, cd, g]
    if state is not None:
        in_specs.append(pl.BlockSpec((None, None, RET_HEADS, RET_DK, RET_DV), lambda b, j: (layer, b, 0, 0, 0)))
        args.append(state)
    return pl.pallas_call(
        functools.partial(_retention_kernel, has_state=state is not None),
        grid=(n_batch, nc), in_specs=in_specs,
        out_specs=[pl.BlockSpec((c, RET_WIDTH), row),
                   pl.BlockSpec((None, RET_HEADS, RET_DK, RET_DV), lambda b, j: (b, 0, 0, 0))],
        out_shape=[jax.ShapeDtypeStruct((n, RET_WIDTH), f32),
                   jax.ShapeDtypeStruct((n_batch, RET_HEADS, RET_DK, RET_DV), f32)],
        compiler_params=_cparams(("arbitrary", "arbitrary")), name="retention",
    )(*args)


def _kth_largest(count_ge, k_sel, shape, bits, lowest):
    def value_pass(i, t):
        cand = t + lax.shift_left(jnp.int32(1), bits - 1 - i)
        return jnp.where(count_ge(cand) >= k_sel, cand, t)

    return lax.fori_loop(0, bits, value_pass, jnp.full(shape, lowest, i32))


def _radix_select(t, count_ge, count_eq_before, k_sel, shape, idx_bits):
    n_gt = count_ge(t + 1)
    need = k_sel - n_gt
    n_eq = count_ge(t) - n_gt
    finite = t > _NEG_INF_KEY
    cut_needed = jnp.max(jnp.where(finite & (need < n_eq), 1, 0))

    def search_cut():
        def index_pass(i, c):
            cand = c + lax.shift_left(jnp.int32(1), idx_bits - 1 - i)
            return jnp.where(count_eq_before(t, cand) < need, cand, c)
        return lax.fori_loop(0, idx_bits, index_pass, jnp.zeros(shape, i32))

    c = lax.cond(cut_needed > 0, search_cut, lambda: jnp.full(shape, 2 ** idx_bits - 1, i32))
    return jnp.maximum(t, _NEG_INF_KEY), jnp.where(finite, c, -1)


def _select_mask(keys, index, t, c):
    tie = jnp.where(index <= c, 0.0, MASK_NEG)
    return jnp.where(keys > t, 0.0, jnp.where(keys == t, tie, MASK_NEG))


def _dsa_prompt_kernel(iqt_ref, iwt_ref, ik_ref, aq_ref, ak_ref, av_ref, o_ref, key_ref, hi_ref, lo_ref, bias_ref,
                       s_ref, m_ref, l_ref, acc_ref, *, k_sel, idx_bits):
    qb = pl.program_id(1)
    blk, kc = Q_BLOCK, KEY_CHUNK
    sub = kc // blk
    nkc = qb // sub + 1
    row_b = lax.broadcasted_iota(i32, (blk, blk), 0)
    col_b = lax.broadcasted_iota(i32, (blk, blk), 1) + qb * blk
    row_c = lax.broadcasted_iota(i32, (kc, blk), 0)
    iw = iwt_ref[...] * (IDX_DIM ** -0.5)

    def score_chunk(c, carry):
        for sb in range(sub):
            off = pl.multiple_of(c * kc + sb * blk, blk)
            ikb = ik_ref[pl.ds(off, blk), :].astype(bf16)
            acc = jnp.zeros((blk, blk), f32)
            for hd in range(IDX_HEADS):
                s = jnp.dot(ikb, iqt_ref[hd * IDX_DIM:(hd + 1) * IDX_DIM, :], preferred_element_type=f32)
                acc = acc + jnp.maximum(s, 0.0) * iw[hd:hd + 1, :]
            causal = (row_b + off) <= col_b
            key = _sortable(jnp.where(causal, acc, -jnp.inf))
            key_ref[c, sb * blk:(sb + 1) * blk, :] = key
            hi_ref[c, sb * blk:(sb + 1) * blk, :] = (key >> 16).astype(i16)
        return carry

    lax.fori_loop(0, nkc, score_chunk, 0)

    def count(pred):
        def body(c, acc):
            m = pred(key_ref[c], row_c + c * kc).astype(i32)
            return acc + jnp.sum(m.reshape(kc // SUBLANES, SUBLANES, blk), axis=0)
        c8 = lax.fori_loop(0, nkc, body, jnp.zeros((SUBLANES, blk), i32))
        return jnp.sum(c8, axis=0, keepdims=True)

    def count16_ge(ref):
        packed = 2 * SUBLANES
        n_acc = 4

        def counter(cand):
            cand16 = cand.astype(i16)

            def body(c, accs):
                w = jnp.where(ref[c] >= cand16, jnp.int16(1), jnp.int16(0))
                accs = list(accs)
                for r in range(kc // packed):
                    accs[r % n_acc] = accs[r % n_acc] + w[r * packed:(r + 1) * packed]
                return tuple(accs)

            accs = lax.fori_loop(0, nkc, body, tuple(jnp.zeros((packed, blk), i16) for _ in range(n_acc)))
            tot = (accs[0] + accs[1]) + (accs[2] + accs[3])
            return jnp.sum(tot.astype(i32), axis=0, keepdims=True)
        return counter

    half_min = -(2 ** 15)
    t_hi = _kth_largest(count16_ge(hi_ref), k_sel, (1, blk), 16, half_min)
    n_above = jnp.where(t_hi == 2 ** 15 - 1, 0, count16_ge(hi_ref)(t_hi + 1))

    def low_chunk(c, carry):
        kk = key_ref[c]
        lo = (kk & 0xFFFF) + half_min
        lo_ref[c] = jnp.where((kk >> 16) == t_hi, lo, half_min).astype(i16)
        return carry

    lax.fori_loop(0, nkc, low_chunk, 0)
    t_lo = _kth_largest(count16_ge(lo_ref), k_sel - n_above, (1, blk), 16, half_min)
    t = t_hi * 65536 + (t_lo - half_min)

    count_ge = lambda cand: count(lambda kk, rows: kk >= cand)
    count_eq_before = lambda t, cand: count(lambda kk, rows: (kk == t) & (rows < cand))
    t, cut = _radix_select(t, count_ge, count_eq_before, k_sel, (1, blk), idx_bits)

    def bias_chunk(c, carry):
        for sb in range(sub):
            kk = key_ref[c, sb * blk:(sb + 1) * blk, :]
            mask = _select_mask(kk, row_b + (c * kc + sb * blk), t, cut)
            bias_ref[c, :, sb * blk:(sb + 1) * blk] = mask.T
        return carry

    lax.fori_loop(0, nkc, bias_chunk, 0)

    def lane_fold(x, op):
        y = x[:, :blk]
        for sb in range(1, sub):
            y = op(y, x[:, sb * blk:(sb + 1) * blk])
        return y

    m_ref[...] = jnp.full(m_ref.shape, MASK_NEG, f32)

    def qk_chunk(c, carry):
        off = pl.multiple_of(c * kc, kc)
        bias = bias_ref[c]
        for hd in range(ATT_HEADS):
            sl = slice(hd * HEAD_DIM, (hd + 1) * HEAD_DIM)
            kh = ak_ref[pl.ds(off, kc), sl]
            s = lax.dot_general(aq_ref[:, sl], kh, _NT, preferred_element_type=f32) * (HEAD_DIM ** -0.5) + bias
            s_ref[c, hd] = s
            m_ref[hd] = jnp.maximum(m_ref[hd], lane_fold(s, jnp.maximum))
        return carry

    lax.fori_loop(0, nkc, qk_chunk, 0)
    m_fin = [jnp.max(m_ref[hd], axis=-1, keepdims=True) for hd in range(ATT_HEADS)]
    l_ref[...] = jnp.zeros(l_ref.shape, f32)
    acc_ref[...] = jnp.zeros(acc_ref.shape, f32)

    def pv_chunk(c, carry):
        off = pl.multiple_of(c * kc, kc)
        for hd in range(ATT_HEADS):
            sl = slice(hd * HEAD_DIM, (hd + 1) * HEAD_DIM)
            p = jnp.exp(s_ref[c, hd] - m_fin[hd])
            l_ref[hd] = l_ref[hd] + lane_fold(p, jnp.add)
            acc_ref[hd] = acc_ref[hd] + jnp.dot(p.astype(bf16), av_ref[pl.ds(off, kc), sl], preferred_element_type=f32)
        return carry

    lax.fori_loop(0, nkc, pv_chunk, 0)
    for hd in range(ATT_HEADS):
        o_ref[:, hd * HEAD_DIM:(hd + 1) * HEAD_DIM] = acc_ref[hd] / jnp.sum(l_ref[hd], axis=-1, keepdims=True)


def _dsa_prompt(iqt, iwt, ik, aq, ak, av, n_batch):
    n = aq.shape[0]
    t = n // n_batch
    assert t % KEY_CHUNK == 0, t
    nqb = t // Q_BLOCK
    nkc = t // KEY_CHUNK
    k_sel = min(TOPK_MAX, t // 4)
    idx_bits = max(1, (t - 1).bit_length())
    qrow = lambda b, q: (b * nqb + q, 0)
    qcol = lambda b, q: (0, b * nqb + q)
    brow = lambda b, q: (b, 0)
    return pl.pallas_call(
        functools.partial(_dsa_prompt_kernel, k_sel=k_sel, idx_bits=idx_bits),
        grid=(n_batch, nqb),
        in_specs=[
            pl.BlockSpec((IDX_HEADS * IDX_DIM, Q_BLOCK), qcol),
            pl.BlockSpec((IDX_HEADS, Q_BLOCK), qcol),
            pl.BlockSpec((t, IDX_DIM), brow),
            pl.BlockSpec((Q_BLOCK, ATT_WIDTH), qrow),
            pl.BlockSpec((t, ATT_WIDTH), brow),
            pl.BlockSpec((t, ATT_WIDTH), brow),
        ],
        out_specs=pl.BlockSpec((Q_BLOCK, ATT_WIDTH), qrow),
        out_shape=jax.ShapeDtypeStruct((n, ATT_WIDTH), f32),
        scratch_shapes=[pltpu.VMEM((nkc, KEY_CHUNK, Q_BLOCK), i32), pltpu.VMEM((nkc, KEY_CHUNK, Q_BLOCK), i16),
                        pltpu.VMEM((nkc, KEY_CHUNK, Q_BLOCK), i16), pltpu.VMEM((nkc, Q_BLOCK, KEY_CHUNK), f32),
                        pltpu.VMEM((nkc, ATT_HEADS, Q_BLOCK, KEY_CHUNK), f32),
                        pltpu.VMEM((ATT_HEADS, Q_BLOCK, Q_BLOCK), f32),
                        pltpu.VMEM((ATT_HEADS, Q_BLOCK, Q_BLOCK), f32),
                        pltpu.VMEM((ATT_HEADS, Q_BLOCK, HEAD_DIM), f32)],
        compiler_params=_cparams(("arbitrary", "arbitrary")), name="dsa_prompt",
    )(iqt, iwt, ik, aq, ak, av)


def _page_scores(iq, iw, ik_page):
    s = jnp.dot(iq, ik_page.astype(bf16), preferred_element_type=f32)
    s = jnp.maximum(s * (IDX_DIM ** -0.5), 0.0) * iw
    return jnp.sum(s.reshape(IDX_HEADS, SAMPLE_Q, PAGE_SIZE), axis=0)


def _idx_scores_kernel(pt_ref, iq_ref, iw_ref, *refs, pages):
    page_refs, o_ref = refs[:pages], refs[pages]
    iq, iw = iq_ref[...], iw_ref[...]
    for i in range(pages):
        o_ref[i] = _page_scores(iq, iw, page_refs[i][...])


def _idx_scores(page_table, iq_s, iw_s, kidx_t, layer, pages):
    db, n_pages = page_table.shape
    nj = n_pages // pages
    page_spec = lambda i: pl.BlockSpec((None, None, IDX_DIM, PAGE_SIZE),
                                       lambda b, j, pt: (layer, pt[b, j * pages + i], 0, 0))
    return pl.pallas_call(
        functools.partial(_idx_scores_kernel, pages=pages),
        grid_spec=pltpu.PrefetchScalarGridSpec(
            num_scalar_prefetch=1, grid=(db, nj),
            in_specs=[pl.BlockSpec((None, IDX_HEADS * SAMPLE_Q, IDX_DIM), lambda b, j, pt: (b, 0, 0)),
                      pl.BlockSpec((None, IDX_HEADS * SAMPLE_Q, LANES), lambda b, j, pt: (b, 0, 0))]
                     + [page_spec(i) for i in range(pages)],
            out_specs=pl.BlockSpec((None, pages, SAMPLE_Q, PAGE_SIZE), lambda b, j, pt: (b, j, 0, 0)),
        ),
        out_shape=jax.ShapeDtypeStruct((db, n_pages, SAMPLE_Q, PAGE_SIZE), f32),
        compiler_params=_cparams(("arbitrary", "arbitrary")), name="idx_scores",
    )(page_table, iq_s, iw_s, *([kidx_t] * pages))


def _select_kernel(sc_ref, iq_ref, iw_ref, ikn_ref, bias_ref, key_ref, *, k_sel, idx_bits):
    n_req, n_pages = sc_ref.shape[0], sc_ref.shape[1]
    rows = n_req * SAMPLE_Q
    nch = n_pages + 1
    lane_i = lax.broadcasted_iota(i32, (rows, LANES), 1)
    q_i = lax.broadcasted_iota(i32, (SAMPLE_Q, LANES), 0)
    l_i = lax.broadcasted_iota(i32, (SAMPLE_Q, LANES), 1)

    def to_key(ch, carry):
        key_ref[ch] = _sortable(jnp.concatenate([sc_ref[r, ch] for r in range(n_req)], axis=0))
        return carry

    lax.fori_loop(0, n_pages, to_key, 0)
    new = []
    for r in range(n_req):
        s = _page_scores(iq_ref[r], iw_ref[r], ikn_ref[r])
        new.append(jnp.where(l_i <= q_i, s, -jnp.inf))
    key_ref[n_pages] = _sortable(jnp.concatenate(new, axis=0))

    def count_ge(cand):
        cand_b = jnp.broadcast_to(cand, (rows, LANES))

        def body(ch, c):
            return c + (key_ref[ch] >= cand_b).astype(i32)
        c = lax.fori_loop(0, nch, body, jnp.zeros((rows, LANES), i32))
        return jnp.sum(c, axis=-1, keepdims=True)

    def count_eq_before(t, cand):
        t_b = jnp.broadcast_to(t, (rows, LANES))
        cand_b = jnp.broadcast_to(cand, (rows, LANES))

        def body(ch, c):
            return c + ((key_ref[ch] == t_b) & ((lane_i + ch * LANES) < cand_b)).astype(i32)
        c = lax.fori_loop(0, nch, body, jnp.zeros((rows, LANES), i32))
        return jnp.sum(c, axis=-1, keepdims=True)

    t = _kth_largest(count_ge, k_sel, (rows, 1), 32, INT_MIN)
    t, c = _radix_select(t, count_ge, count_eq_before, k_sel, (rows, 1), idx_bits)
    t_b = jnp.broadcast_to(t, (rows, LANES))
    c_b = jnp.broadcast_to(c, (rows, LANES))

    def to_bias(ch, carry):
        bias = _select_mask(key_ref[ch], lane_i + ch * LANES, t_b, c_b)
        for r in range(n_req):
            bias_ref[r, ch] = bias[r * SAMPLE_Q:(r + 1) * SAMPLE_Q]
        return carry

    lax.fori_loop(0, nch, to_bias, 0)


def _select(scores, iq_s, iw_s, ik_new, k_sel, n_req):
    db, n_pages = scores.shape[0], scores.shape[1]
    nch = n_pages + 1
    idx_bits = max(1, (nch * LANES - 1).bit_length())
    req3 = lambda g: (g, 0, 0)
    req4 = lambda g: (g, 0, 0, 0)
    return pl.pallas_call(
        functools.partial(_select_kernel, k_sel=k_sel, idx_bits=idx_bits),
        grid=(db // n_req,),
        in_specs=[pl.BlockSpec((n_req, n_pages, SAMPLE_Q, LANES), req4),
                  pl.BlockSpec((n_req, IDX_HEADS * SAMPLE_Q, IDX_DIM), req3),
                  pl.BlockSpec((n_req, IDX_HEADS * SAMPLE_Q, LANES), req3),
                  pl.BlockSpec((n_req, IDX_DIM, PAGE_SIZE), req3)],
        out_specs=pl.BlockSpec((n_req, nch, SAMPLE_Q, LANES), req4),
        out_shape=jax.ShapeDtypeStruct((db, nch, SAMPLE_Q, LANES), f32),
        scratch_shapes=[pltpu.VMEM((nch, n_req * SAMPLE_Q, LANES), i32)],
        compiler_params=_cparams(("arbitrary",)), name="select",
    )(scores, iq_s, iw_s, ik_new)


def _paged_attn_kernel(pt_ref, q_ref, hm_ref, ex_ref, bias_ref, biasn_ref, kn_ref, vn_ref, *refs, pages):
    k_refs, v_refs = refs[:pages], refs[pages:2 * pages]
    o_ref, q_s, m_ref, l_ref, acc_ref = refs[2 * pages:]
    j = pl.program_id(1)
    rows = ATT_HEADS * SAMPLE_Q

    @pl.when(j == 0)
    def _():
        qf = q_ref[...].astype(f32)
        q_s[...] = jnp.concatenate(
            [qf[:SAMPLE_Q, hd * HEAD_DIM:(hd + 1) * HEAD_DIM] for hd in range(ATT_HEADS)], axis=0).astype(bf16)
        m_ref[...] = jnp.full(m_ref.shape, MASK_NEG, f32)
        l_ref[...] = jnp.zeros(l_ref.shape, f32)
        acc_ref[...] = jnp.zeros(acc_ref.shape, f32)

    q = q_s[...]
    hm = hm_ref[...]

    def scores(k_page, b8):
        s = lax.dot_general(q, k_page.astype(bf16), _NT, preferred_element_type=f32) * (HEAD_DIM ** -0.5)
        sel = jnp.where(b8 == 0.0, 1.0, 0.0)
        sel = jnp.concatenate([sel] * ATT_HEADS, axis=0).astype(bf16)
        selx = jnp.dot(sel, ex_ref[...], preferred_element_type=f32) * hm
        return jnp.where(selx > 0.5, s, MASK_NEG)

    def accumulate(s_list, v_list):
        m_old = m_ref[...]
        mx = s_list[0]
        for s in s_list[1:]:
            mx = jnp.maximum(mx, s)
        m_new = jnp.maximum(m_old, jnp.max(mx, axis=-1, keepdims=True))
        alpha = jnp.exp(m_old - m_new)
        psum = jnp.zeros((rows, ATT_HEADS * PAGE_SIZE), f32)
        pv = jnp.zeros((rows, HEAD_DIM), f32)
        for s, v in zip(s_list, v_list):
            p = jnp.exp(s - m_new)
            psum = psum + p
            pv = pv + jnp.dot(p.astype(bf16), v.astype(bf16), preferred_element_type=f32)
        l_ref[...] = alpha * l_ref[...] + jnp.sum(psum, axis=-1, keepdims=True)
        acc_ref[...] = alpha * acc_ref[...] + pv
        m_ref[...] = m_new

    accumulate([scores(k_refs[i][...], bias_ref[i]) for i in range(pages)],
               [v_refs[i][...] for i in range(pages)])

    @pl.when(j == pl.num_programs(1) - 1)
    def _():
        accumulate([scores(kn_ref[...], biasn_ref[...])], [vn_ref[...]])
        o = acc_ref[...] / l_ref[...]
        o_ref[...] = jnp.zeros(o_ref.shape, f32)
        for hd in range(ATT_HEADS):
            o_ref[:SAMPLE_Q, hd * HEAD_DIM:(hd + 1) * HEAD_DIM] = o[hd * SAMPLE_Q:(hd + 1) * SAMPLE_Q]


def _paged_attn(page_table, aq_s, bias, k_new, v_new, cache_k, cache_v, layer, pages):
    db, n_pages = page_table.shape
    nj = n_pages // pages
    page_rows = PAGE_SIZE * ATT_HEADS
    rows = ATT_HEADS * SAMPLE_Q
    col = jnp.arange(page_rows)
    head_match = (col[None, :] % ATT_HEADS == jnp.arange(rows)[:, None] // SAMPLE_Q).astype(f32)
    expand = (col[None, :] // ATT_HEADS == jnp.arange(PAGE_SIZE)[:, None]).astype(bf16)
    page_spec = lambda i: pl.BlockSpec((None, None, page_rows, HEAD_DIM),
                                       lambda b, j, pt: (layer, pt[b, j * pages + i], 0, 0))
    req = lambda b, j, pt: (b, 0)
    const = lambda b, j, pt: (0, 0)
    return pl.pallas_call(
        functools.partial(_paged_attn_kernel, pages=pages),
        grid_spec=pltpu.PrefetchScalarGridSpec(
            num_scalar_prefetch=1, grid=(db, nj),
            in_specs=[pl.BlockSpec((SAMPLE_ROWS, ATT_WIDTH), req),
                      pl.BlockSpec(head_match.shape, const),
                      pl.BlockSpec(expand.shape, const),
                      pl.BlockSpec((None, pages, SAMPLE_Q, LANES), lambda b, j, pt: (b, j, 0, 0)),
                      pl.BlockSpec((None, None, SAMPLE_Q, LANES), lambda b, j, pt: (b, n_pages, 0, 0)),
                      pl.BlockSpec((page_rows, HEAD_DIM), req),
                      pl.BlockSpec((page_rows, HEAD_DIM), req)]
                     + [page_spec(i) for i in range(pages)] * 2,
            out_specs=pl.BlockSpec((SAMPLE_ROWS, ATT_WIDTH), req),
            scratch_shapes=[pltpu.VMEM((rows, HEAD_DIM), bf16),
                            pltpu.VMEM((rows, 1), f32),
                            pltpu.VMEM((rows, 1), f32),
                            pltpu.VMEM((rows, HEAD_DIM), f32)],
        ),
        out_shape=jax.ShapeDtypeStruct((db * SAMPLE_ROWS, ATT_WIDTH), f32),
        compiler_params=_cparams(("arbitrary", "arbitrary")), name="paged_attn",
    )(page_table, aq_s, head_match, expand, bias, bias, k_new, v_new, *([cache_k] * pages), *([cache_v] * pages))


def _merge_kernel(x_ref, oret_ref, oatt_ref, ng_ref, wg_ref, wbr_ref, wba_ref, wo_ref, y_ref):
    x = x_ref[...]
    ms = jnp.mean(x * x, axis=-1, keepdims=True)
    h = ((x * lax.rsqrt(ms + EPS)) * ng_ref[...]).astype(bf16)
    g = jnp.dot(h, wg_ref[...], preferred_element_type=f32)
    rg = g[:, 0:RET_WIDTH]
    ag = g[:, RET_WIDTH:RET_WIDTH + ATT_WIDTH]
    ga = g[:, RET_WIDTH + ATT_WIDTH:RET_WIDTH + ATT_WIDTH + D_MODEL]
    gb = g[:, RET_WIDTH + ATT_WIDTH + D_MODEL:]
    o_ret = oret_ref[...] * (rg * jax.nn.sigmoid(rg))
    o_att = oatt_ref[...] * (ag * jax.nn.sigmoid(ag))
    a = jnp.dot(o_ret.astype(bf16), wbr_ref[...], preferred_element_type=f32)
    b = jnp.dot(o_att.astype(bf16), wba_ref[...], preferred_element_type=f32)
    y = jax.nn.sigmoid(ga) * a + jax.nn.sigmoid(gb) * b
    y_ref[...] = x + jnp.dot(y.astype(bf16), wo_ref[...], preferred_element_type=f32)


def _merge(x2d, o_ret, o_att, w, tm):
    n = x2d.shape[0]
    row = lambda i: (i, 0)
    full = lambda a: pl.BlockSpec(a.shape, lambda i: (0, 0))
    return pl.pallas_call(
        _merge_kernel, grid=(n // tm,),
        in_specs=[pl.BlockSpec((tm, D_MODEL), row), pl.BlockSpec((tm, RET_WIDTH), row),
                  pl.BlockSpec((tm, ATT_WIDTH), row),
                  full(w["ng"]), full(w["wg"]), full(w["wbr"]), full(w["wba"]), full(w["wo"])],
        out_specs=pl.BlockSpec((tm, D_MODEL), row),
        out_shape=jax.ShapeDtypeStruct((n, D_MODEL), f32),
        compiler_params=_cparams(("arbitrary",)), name="merge",
    )(x2d, o_ret, o_att, w["ng"], w["wg"], w["wbr"], w["wba"], w["wo"])


def _prep_weights(l, norm_g, w_in, q_norm_g, k_norm_g, kidx_norm_g, w_br_ret, w_br_att, w_out):
    w = w_in[l]
    pad_cols = lambda a, n: jnp.pad(a, ((0, 0), (0, n - a.shape[1])))
    wiq = jnp.concatenate([w[:, _O_IQ:_O_IK], w[:, _O_IW:_O_GA]], axis=1).T
    wiq = jnp.pad(wiq, ((0, 2 * SUBLANES - IDX_HEADS), (0, 0)))
    return {
        "ng": norm_g[l].reshape(1, D_MODEL),
        "wr": w[:, _O_RQ:_O_RG].astype(bf16),
        "wa": jnp.concatenate([w[:, _O_AQ:_O_AG]], axis=1).astype(bf16),
        "wiq": wiq.astype(bf16),
        "wik": pad_cols(w[:, _O_IK:_O_IW], LANES).astype(bf16),
        "qg": q_norm_g[l].reshape(1, HEAD_DIM),
        "kg": k_norm_g[l].reshape(1, HEAD_DIM),
        "ig": pad_cols(kidx_norm_g[l].reshape(1, IDX_DIM), LANES),
        "wg": jnp.concatenate([w[:, _O_RG:_O_AQ], w[:, _O_AG:_O_IQ], w[:, _O_GA:_O_END]], axis=1).astype(bf16),
        "wbr": w_br_ret[l].astype(bf16),
        "wba": w_br_att[l].astype(bf16),
        "wo": w_out[l].astype(bf16),
    }


def _position_tables(pos):
    rc, rs, _, _ = _rope_tables(pos, RET_DK, RET_ROPE_THETA, LANES)
    ac, as_, _, _ = _rope_tables(pos, ROPE_DIM, ROPE_THETA, LANES)
    ic, is_, ict, ist = _rope_tables(pos, IDX_ROPE_DIM, ROPE_THETA, LANES)
    return rc, rs, ac, as_, ic, is_, ict, ist


def _token_tile(n):
    for tm in (256, 128):
        if n % tm == 0:
            return tm
    assert n % SAMPLE_ROWS == 0 and n < LANES, n
    return n


def _prompt_layer(x2d, n_batch, w, ret_g, tabs, tm):
    rq, rk, rv, aq, ak, av, akb, avb, iqt, ik, iwt = _inproj(x2d, w, tabs, tm)
    o_ret, s_new = _retention(rq, rk, rv, ret_g, n_batch, RET_CHUNK)
    o_att = _dsa_prompt(iqt, iwt, ik, aq, akb, avb, n_batch)
    y = _merge(x2d, o_ret, o_att, w, tm)
    return y, ak, av, ik, s_new


def _sample_layer(xs, n_true, w, ret_g, tabs, tm, layer, state_ret, page_table, cache_k, cache_v, kidx_t, pages):
    db = xs.shape[0]
    n = db * SAMPLE_ROWS
    past = page_table.shape[1] * PAGE_SIZE
    x2d = xs.reshape(n, D_MODEL)
    rq, rk, rv, aq, ak, av, _, _, iqt, ik, iwt = _inproj(x2d, w, tabs, tm)

    padc = lambda a: jnp.pad(a.reshape(db, SAMPLE_ROWS, -1), ((0, 0), (0, RET_CHUNK - SAMPLE_ROWS), (0, 0))
                             ).reshape(db * RET_CHUNK, -1)
    o_ret, s_new = _retention(padc(rq), padc(rk), padc(rv), ret_g, db, n_true, state=state_ret, layer=layer)
    o_ret = o_ret.reshape(db, RET_CHUNK, RET_WIDTH)[:, :SAMPLE_ROWS].reshape(n, RET_WIDTH)

    iq_s = iqt.reshape(IDX_HEADS, IDX_DIM, db, SAMPLE_ROWS)[:, :, :, :SAMPLE_Q]
    iq_s = iq_s.transpose(2, 0, 3, 1).reshape(db, IDX_HEADS * SAMPLE_Q, IDX_DIM)
    iw_s = iwt.reshape(IDX_HEADS, db, SAMPLE_ROWS)[:, :, :SAMPLE_Q].transpose(1, 0, 2)
    iw_s = jnp.broadcast_to(iw_s.reshape(db, IDX_HEADS * SAMPLE_Q, 1), (db, IDX_HEADS * SAMPLE_Q, LANES))
    ik_new = jnp.pad(ik.reshape(db, SAMPLE_ROWS, IDX_DIM), ((0, 0), (0, PAGE_SIZE - SAMPLE_ROWS), (0, 0)))
    ik_new = ik_new.transpose(0, 2, 1)

    scores = _idx_scores(page_table, iq_s, iw_s, kidx_t, layer, pages)
    k_sel = min(TOPK_MAX, (past + n_true) // 4)
    n_req = 8 if db % 8 == 0 else db
    bias = _select(scores, iq_s, iw_s, ik_new, k_sel, n_req)

    pad_new = lambda a: jnp.pad(a.reshape(db, SAMPLE_ROWS * ATT_HEADS, HEAD_DIM),
                                ((0, 0), (0, (PAGE_SIZE - SAMPLE_ROWS) * ATT_HEADS), (0, 0))
                                ).reshape(db * PAGE_SIZE * ATT_HEADS, HEAD_DIM)
    o_att = _paged_attn(page_table, aq, bias, pad_new(ak), pad_new(av), cache_k, cache_v, layer, pages)
    y = _merge(x2d, o_ret, o_att, w, tm)
    return y.reshape(db, SAMPLE_ROWS, D_MODEL), ak, av, ik, s_new


def kernel(x_prompt, x_sample, cache_k, cache_v, cache_kidx, state_ret, page_table, norm_g, w_in, q_norm_g,
           k_norm_g, kidx_norm_g, ret_norm_g, w_br_ret, w_br_att, w_out):
    depth = w_in.shape[0]
    n_batch, t, _ = x_prompt.shape
    db, ds, _ = x_sample.shape
    n_pages = page_table.shape[1]
    past = n_pages * PAGE_SIZE
    assert t % RET_CHUNK == 0 and ds <= SAMPLE_Q

    tm_p = _token_tile(t)
    tabs_p = _position_tables(jnp.arange(t))
    n_s = db * SAMPLE_ROWS
    tm_s = _token_tile(n_s)
    pos_s = past + (jnp.arange(tm_s) % SAMPLE_ROWS)
    tabs_s = _position_tables(pos_s)
    pages = 16 if n_pages % 16 == 0 else n_pages

    n_pool = cache_k.shape[1]
    cache_k = cache_k.reshape(depth, n_pool, PAGE_SIZE * ATT_HEADS, HEAD_DIM)
    cache_v = cache_v.reshape(depth, n_pool, PAGE_SIZE * ATT_HEADS, HEAD_DIM)
    kidx_t = jnp.swapaxes(cache_kidx, 2, 3)

    xp = x_prompt.reshape(n_batch * t, D_MODEL)
    pad_s = lambda a: jnp.pad(a, ((0, 0), (0, SAMPLE_ROWS - ds), (0, 0)))
    xs = pad_s(x_sample)
    kp, vp, ip, sp, kd, vd, idd, sd = [], [], [], [], [], [], [], []
    for l in range(depth):
        w = _prep_weights(l, norm_g, w_in, q_norm_g, k_norm_g, kidx_norm_g, w_br_ret, w_br_att, w_out)
        xp, k_new, v_new, i_new, s_new = _prompt_layer(xp, n_batch, w, ret_norm_g[l], tabs_p, tm_p)
        kp.append(k_new.reshape(n_batch, t, ATT_HEADS, HEAD_DIM))
        vp.append(v_new.reshape(n_batch, t, ATT_HEADS, HEAD_DIM))
        ip.append(i_new.reshape(n_batch, t, IDX_DIM))
        sp.append(s_new)
        ys, k_new, v_new, i_new, s_new = _sample_layer(
            xs, ds, w, ret_norm_g[l], tabs_s, tm_s, l, state_ret, page_table, cache_k, cache_v, kidx_t, pages)
        xs = pad_s(ys[:, :ds])
        kd.append(k_new.reshape(db, SAMPLE_ROWS, ATT_HEADS, HEAD_DIM)[:, :ds])
        vd.append(v_new.reshape(db, SAMPLE_ROWS, ATT_HEADS, HEAD_DIM)[:, :ds])
        idd.append(i_new.reshape(db, SAMPLE_ROWS, IDX_DIM)[:, :ds])
        sd.append(s_new)
    return (xp.reshape(n_batch, t, D_MODEL), xs[:, :ds],
            jnp.stack(kp), jnp.stack(vp), jnp.stack(ip), jnp.stack(sp),
            jnp.stack(kd), jnp.stack(vd), jnp.stack(idd), jnp.stack(sd))
```

```python
import functools
import math

import jax
import jax.numpy as jnp
from jax import lax
from jax.experimental import pallas as pl
from jax.experimental.pallas import tpu as pltpu

D_MODEL = 1024
PAGE_SIZE = 128
RET_HEADS = 4
RET_DK = 128
RET_DV = 128
RET_WIDTH = RET_HEADS * RET_DV
RET_CHUNK = 128
RET_ROPE_THETA = 10000.0
ATT_HEADS = 4
HEAD_DIM = 128
ATT_WIDTH = ATT_HEADS * HEAD_DIM
ROPE_DIM = HEAD_DIM // 4
ROPE_THETA = 500000.0
IDX_HEADS = 8
IDX_DIM = 64
IDX_ROPE_DIM = IDX_DIM // 4
TOPK_MAX = 256
Q_BLOCK = 128
EPS = 1e-6

LANES = 128
SUBLANES = 8
VMEM_LIMIT = 56 * 1024 * 1024

_O_RQ, _O_RK, _O_RV, _O_RG = 0, 512, 1024, 1536
_O_AQ, _O_AK, _O_AV, _O_AG = 2048, 2560, 3072, 3584
_O_IQ, _O_IK, _O_IW = 4096, 4608, 4672
_O_GA, _O_GB, _O_END = 4680, 5704, 6728

KEY_CHUNK = 512
SAMPLE_ROWS = 16
SAMPLE_Q = 8
MASK_NEG = -1e30

f32 = jnp.float32
bf16 = jnp.bfloat16
i32 = jnp.int32
i16 = jnp.int16

_NT = (((1,), (1,)), ((), ()))


def _cparams(sem, flags=None):
    return pltpu.CompilerParams(dimension_semantics=sem, vmem_limit_bytes=VMEM_LIMIT, flags=flags)


def _sortable(s):
    b = pltpu.bitcast(s + 0.0, i32)
    return b ^ ((b >> 31) & jnp.int32(0x7FFFFFFF))


_NEG_INF_KEY = -2139095041


def _rope_tables(pos, rot_dim, theta, width):
    half = rot_dim // 2
    freqs = jnp.exp(-jnp.arange(half, dtype=f32) * (2.0 / rot_dim) * math.log(theta))
    ang = pos.astype(f32)[:, None] * freqs[None, :]
    cos, sin = jnp.cos(ang), jnp.sin(ang)
    t = pos.shape[0]
    c = jnp.concatenate([cos, cos, jnp.ones((t, width - rot_dim), f32)], axis=-1)
    s = jnp.concatenate([sin, -sin, jnp.zeros((t, width - rot_dim), f32)], axis=-1)
    return c, s, cos.T, sin.T


def _rope_lanes(x, c, s, half, rot, lane):
    w = x * s
    if rot == LANES:
        return x * c + pltpu.roll(w, half, 1)
    up = pltpu.roll(w, LANES - half, 1)
    dn = pltpu.roll(w, half, 1)
    return x * c + jnp.where(lane < half, up, jnp.where(lane < rot, dn, 0.0))


def _inproj_kernel(x_ref, ng_ref, wr_ref, wa_ref, wiq_ref, wik_ref, qg_ref, kg_ref, ig_ref,
                   rc_ref, rs_ref, ac_ref, as_ref, ic_ref, is_ref, ict_ref, ist_ref,
                   rq_ref, rk_ref, rv_ref, aq_ref, ak_ref, av_ref, akb_ref, avb_ref, iqt_ref, ik_ref, iwt_ref):
    x = x_ref[...]
    tm = x.shape[0]
    ms = jnp.mean(x * x, axis=-1, keepdims=True)
    h = ((x * lax.rsqrt(ms + EPS)) * ng_ref[...]).astype(bf16)
    lane = lax.broadcasted_iota(i32, (tm, LANES), 1)

    r = jnp.dot(h, wr_ref[...], preferred_element_type=f32)
    rc, rs = rc_ref[...], rs_ref[...]
    for hd in range(RET_HEADS):
        sl = slice(hd * RET_DK, (hd + 1) * RET_DK)
        q = r[:, _O_RQ + hd * RET_DK:_O_RQ + (hd + 1) * RET_DK]
        k = r[:, _O_RK + hd * RET_DK:_O_RK + (hd + 1) * RET_DK]
        rq_ref[:, sl] = _rope_lanes(q, rc, rs, RET_DK // 2, RET_DK, lane).astype(bf16)
        rk_ref[:, sl] = (_rope_lanes(k, rc, rs, RET_DK // 2, RET_DK, lane) * (RET_DK ** -0.5)).astype(bf16)
    rv_ref[...] = r[:, _O_RV:_O_RV + RET_WIDTH].astype(bf16)

    a = jnp.dot(h, wa_ref[...], preferred_element_type=f32)
    ac, as_ = ac_ref[...], as_ref[...]
    for hd in range(ATT_HEADS):
        sl = slice(hd * HEAD_DIM, (hd + 1) * HEAD_DIM)
        q = a[:, hd * HEAD_DIM:(hd + 1) * HEAD_DIM]
        k = a[:, ATT_WIDTH + hd * HEAD_DIM:ATT_WIDTH + (hd + 1) * HEAD_DIM]
        q = (q * lax.rsqrt(jnp.mean(q * q, axis=-1, keepdims=True) + EPS)) * qg_ref[...]
        k = (k * lax.rsqrt(jnp.mean(k * k, axis=-1, keepdims=True) + EPS)) * kg_ref[...]
        aq_ref[:, sl] = _rope_lanes(q, ac, as_, ROPE_DIM // 2, ROPE_DIM, lane).astype(bf16)
        k = _rope_lanes(k, ac, as_, ROPE_DIM // 2, ROPE_DIM, lane)
        ak_ref[:, hd, :] = k
        akb_ref[:, sl] = k.astype(bf16)
        av_ref[:, hd, :] = a[:, 2 * ATT_WIDTH + hd * HEAD_DIM:2 * ATT_WIDTH + (hd + 1) * HEAD_DIM]
    avb_ref[...] = a[:, 2 * ATT_WIDTH:3 * ATT_WIDTH].astype(bf16)

    pk = jnp.dot(h, wik_ref[...], preferred_element_type=f32)
    mk = jnp.sum(pk * pk, axis=-1, keepdims=True) * (1.0 / IDX_DIM)
    pk = (pk * lax.rsqrt(mk + EPS)) * ig_ref[...]
    pk = _rope_lanes(pk, ic_ref[...], is_ref[...], IDX_ROPE_DIM // 2, IDX_ROPE_DIM, lane)
    ik_ref[...] = pk[:, :IDX_DIM]

    t = lax.dot_general(wiq_ref[...], h, _NT, preferred_element_type=f32)
    ct, st = ict_ref[...], ist_ref[...]
    hr = IDX_ROPE_DIM // 2
    for hd in range(IDX_HEADS):
        base = hd * IDX_DIM
        x1 = t[base:base + hr]
        x2 = t[base + hr:base + 2 * hr]
        blk = jnp.concatenate([x1 * ct - x2 * st, x1 * st + x2 * ct, t[base + 2 * hr:base + IDX_DIM]], axis=0)
        iqt_ref[base:base + IDX_DIM, :] = blk.astype(bf16)
    iwt_ref[...] = t[IDX_HEADS * IDX_DIM:IDX_HEADS * IDX_DIM + IDX_HEADS] * (IDX_HEADS ** -0.5)


def _inproj(x2d, w, tabs, tm):
    n = x2d.shape[0]
    rc, rs, ac, as_, ic, is_, ict, ist = tabs
    tt = rc.shape[0]
    nj, nr = tt // tm, n // tt
    row = lambda j, r: (r * nj + j, 0)
    row3 = lambda j, r: (r * nj + j, 0, 0)
    col = lambda j, r: (0, r * nj + j)
    tab = lambda j, r: (j, 0)
    tabt = lambda j, r: (0, j)
    const = lambda j, r: (0, 0)
    full = lambda a: pl.BlockSpec(a.shape, const)
    in_specs = [
        pl.BlockSpec((tm, D_MODEL), row),
        full(w["ng"]), full(w["wr"]), full(w["wa"]), full(w["wiq"]), full(w["wik"]),
        full(w["qg"]), full(w["kg"]), full(w["ig"]),
        pl.BlockSpec((tm, LANES), tab), pl.BlockSpec((tm, LANES), tab),
        pl.BlockSpec((tm, LANES), tab), pl.BlockSpec((tm, LANES), tab),
        pl.BlockSpec((tm, LANES), tab), pl.BlockSpec((tm, LANES), tab),
        pl.BlockSpec((IDX_ROPE_DIM // 2, tm), tabt), pl.BlockSpec((IDX_ROPE_DIM // 2, tm), tabt),
    ]
    out_shape = [
        jax.ShapeDtypeStruct((n, RET_WIDTH), bf16), jax.ShapeDtypeStruct((n, RET_WIDTH), bf16),
        jax.ShapeDtypeStruct((n, RET_WIDTH), bf16), jax.ShapeDtypeStruct((n, ATT_WIDTH), bf16),
        jax.ShapeDtypeStruct((n, ATT_HEADS, HEAD_DIM), f32), jax.ShapeDtypeStruct((n, ATT_HEADS, HEAD_DIM), f32),
        jax.ShapeDtypeStruct((n, ATT_WIDTH), bf16), jax.ShapeDtypeStruct((n, ATT_WIDTH), bf16),
        jax.ShapeDtypeStruct((IDX_HEADS * IDX_DIM, n), bf16),
        jax.ShapeDtypeStruct((n, IDX_DIM), f32),
        jax.ShapeDtypeStruct((IDX_HEADS, n), f32),
    ]
    out_specs = [
        pl.BlockSpec((tm, RET_WIDTH), row), pl.BlockSpec((tm, RET_WIDTH), row),
        pl.BlockSpec((tm, RET_WIDTH), row), pl.BlockSpec((tm, ATT_WIDTH), row),
        pl.BlockSpec((tm, ATT_HEADS, HEAD_DIM), row3), pl.BlockSpec((tm, ATT_HEADS, HEAD_DIM), row3),
        pl.BlockSpec((tm, ATT_WIDTH), row), pl.BlockSpec((tm, ATT_WIDTH), row),
        pl.BlockSpec((IDX_HEADS * IDX_DIM, tm), col),
        pl.BlockSpec((tm, IDX_DIM), row),
        pl.BlockSpec((IDX_HEADS, tm), col),
    ]
    return pl.pallas_call(
        _inproj_kernel, grid=(nj, nr), in_specs=in_specs, out_specs=out_specs, out_shape=out_shape,
        compiler_params=_cparams(("arbitrary", "arbitrary")), name="inproj",
    )(x2d, w["ng"], w["wr"], w["wa"], w["wiq"], w["wik"], w["qg"], w["kg"], w["ig"],
      rc, rs, ac, as_, ic, is_, ict, ist)


def _retention_kernel(*refs, has_state):
    if has_state:
        rq_ref, rk_ref, rv_ref, dm_ref, qd_ref, kd_ref, cd_ref, g_ref, s0_ref, o_ref, st_ref = refs
    else:
        rq_ref, rk_ref, rv_ref, dm_ref, qd_ref, kd_ref, cd_ref, g_ref, o_ref, st_ref = refs

    @pl.when(pl.program_id(1) == 0)
    def _():
        if has_state:
            st_ref[...] = s0_ref[...]
        else:
            st_ref[...] = jnp.zeros(st_ref.shape, f32)

    for hd in range(RET_HEADS):
        sl = slice(hd * RET_DK, (hd + 1) * RET_DK)
        q, k, v = rq_ref[:, sl], rk_ref[:, sl], rv_ref[:, sl]
        st = st_ref[hd]
        s = lax.dot_general(q, k, _NT, preferred_element_type=f32) * dm_ref[hd]
        inner = jnp.dot(s.astype(bf16), v, preferred_element_type=f32)
        qd = (q.astype(f32) * qd_ref[hd]).astype(bf16)
        cross = jnp.dot(qd, st.astype(bf16), preferred_element_type=f32)
        kdt = (k.astype(f32) * kd_ref[hd]).T.astype(bf16)
        st_ref[hd] = cd_ref[hd] * st + jnp.dot(kdt, v, preferred_element_type=f32)
        o = inner + cross
        o = (o * lax.rsqrt(jnp.mean(o * o, axis=-1, keepdims=True) + EPS)) * g_ref[hd]
        o_ref[:, sl] = o


def _ret_log_decay():
    return jnp.log1p(-jnp.exp2(-5.0 - jnp.arange(RET_HEADS, dtype=f32)))


def _retention_tables(c_block, c_true):
    log_g = _ret_log_decay()
    i = jnp.arange(c_block, dtype=f32)
    diff = i[:, None] - i[None, :]
    dmask = jnp.where(diff >= 0, jnp.exp(log_g[:, None, None] * jnp.maximum(diff, 0.0)), 0.0)
    q_dec = jnp.exp(log_g[:, None] * (i + 1.0)[None, :])
    k_dec = jnp.where(i[None, :] < c_true, jnp.exp(log_g[:, None] * (c_true - 1.0 - i)[None, :]), 0.0)
    chunk_decay = jnp.exp(log_g * c_true)
    bc = lambda a: jnp.broadcast_to(a[:, :, None], (RET_HEADS, c_block, LANES))
    cd = jnp.broadcast_to(chunk_decay[:, None, None], (RET_HEADS, 1, LANES))
    return dmask, bc(q_dec), bc(k_dec), cd


def _retention(rq, rk, rv, ret_g, n_batch, c_true, state=None, layer=0):
    n = rq.shape[0]
    c = RET_CHUNK
    nc = n // n_batch // c
    dm, qd, kd, cd = _retention_tables(c, c_true)
    g = ret_g.reshape(RET_HEADS, 1, RET_DV)
    row = lambda b, j: (b * nc + j, 0)
    c3 = lambda b, j: (0, 0, 0)
    in_specs = [
        pl.BlockSpec((c, RET_WIDTH), row), pl.BlockSpec((c, RET_WIDTH), row), pl.BlockSpec((c, RET_WIDTH), row),
        pl.BlockSpec(dm.shape, c3), pl.BlockSpec(qd.shape, c3), pl.BlockSpec(kd.shape, c3),
        pl.BlockSpec(cd.shape, c3), pl.BlockSpec(g.shape, c3),
    ]
    args = [rq, rk, rv, dm, qd, kd, cd, g]
    if state is not None:
        in_specs.append(pl.BlockSpec((None, None, RET_HEADS, RET_DK, RET_DV), lambda b, j: (layer, b, 0, 0, 0)))
        args.append(state)
    return pl.pallas_call(
        functools.partial(_retention_kernel, has_state=state is not None),
        grid=(n_batch, nc), in_specs=in_specs,
        out_specs=[pl.BlockSpec((c, RET_WIDTH), row),
                   pl.BlockSpec((None, RET_HEADS, RET_DK, RET_DV), lambda b, j: (b, 0, 0, 0))],
        out_shape=[jax.ShapeDtypeStruct((n, RET_WIDTH), f32),
                   jax.ShapeDtypeStruct((n_batch, RET_HEADS, RET_DK, RET_DV), f32)],
        compiler_params=_cparams(("arbitrary", "arbitrary")), name="retention",
    )(*args)


def _kth_largest(count_ge, k_sel, shape, bits, lowest):
    def value_pass(i, t):
        cand = t + lax.shift_left(jnp.int32(1), bits - 1 - i)
        return jnp.where(count_ge(cand) >= k_sel, cand, t)

    return lax.fori_loop(0, bits, value_pass, jnp.full(shape, lowest, i32))


HALF_MIN = -(2 ** 15)


def _high_half(keys):
    return (keys >> 16).astype(i16)


def _low_half(keys, t_hi):
    return jnp.where((keys >> 16) == t_hi, (keys & 0xFFFF) + HALF_MIN, HALF_MIN).astype(i16)


def _kth_largest_packed(count_hi, build_low, count_lo, k_sel, shape):
    t_hi = _kth_largest(count_hi, k_sel, shape, 16, HALF_MIN)
    n_above = jnp.where(t_hi == 2 ** 15 - 1, 0, count_hi(t_hi + 1))
    build_low(t_hi)
    t_lo = _kth_largest(count_lo, k_sel - n_above, shape, 16, HALF_MIN)
    return t_hi * 65536 + (t_lo - HALF_MIN)


def _radix_select(t, count_ge, count_eq_before, k_sel, shape, idx_bits):
    n_gt = count_ge(t + 1)
    need = k_sel - n_gt
    n_eq = count_ge(t) - n_gt
    finite = t > _NEG_INF_KEY
    cut_needed = jnp.max(jnp.where(finite & (need < n_eq), 1, 0))

    def search_cut():
        def index_pass(i, c):
            cand = c + lax.shift_left(jnp.int32(1), idx_bits - 1 - i)
            return jnp.where(count_eq_before(t, cand) < need, cand, c)
        return lax.fori_loop(0, idx_bits, index_pass, jnp.zeros(shape, i32))

    c = lax.cond(cut_needed > 0, search_cut, lambda: jnp.full(shape, 2 ** idx_bits - 1, i32))
    return jnp.maximum(t, _NEG_INF_KEY), jnp.where(finite, c, -1)


def _select_mask(keys, index, t, c):
    tie = jnp.where(index <= c, 0.0, MASK_NEG)
    return jnp.where(keys > t, 0.0, jnp.where(keys == t, tie, MASK_NEG))


def _dsa_prompt_kernel(iqt_ref, iwt_ref, ik_ref, aq_ref, ak_ref, av_ref, o_ref, key_ref, hi_ref, lo_ref, bias_ref,
                       s_ref, m_ref, l_ref, acc_ref, *, k_sel, idx_bits):
    qb = pl.program_id(1)
    sub = KEY_CHUNK // Q_BLOCK
    step = functools.partial(_dsa_prompt_step, iqt_ref, iwt_ref, ik_ref, aq_ref, ak_ref, av_ref, o_ref, key_ref,
                             hi_ref, lo_ref, bias_ref, s_ref, m_ref, l_ref, acc_ref, qb, k_sel, idx_bits)
    lax.switch(qb // sub, [functools.partial(step, n + 1) for n in range(key_ref.shape[0])])


def _dsa_prompt_step(iqt_ref, iwt_ref, ik_ref, aq_ref, ak_ref, av_ref, o_ref, key_ref, hi_ref, lo_ref, bias_ref,
                     s_ref, m_ref, l_ref, acc_ref, qb, k_sel, idx_bits, nkc):
    blk, kc = Q_BLOCK, KEY_CHUNK
    sub = kc // blk

    def chunks(body, init):
        carry = init
        for c in range(nkc):
            carry = body(c, carry)
        return carry

    row_b = lax.broadcasted_iota(i32, (blk, blk), 0)
    col_b = lax.broadcasted_iota(i32, (blk, blk), 1) + qb * blk
    row_c = lax.broadcasted_iota(i32, (kc, blk), 0)
    iw = iwt_ref[...] * (IDX_DIM ** -0.5)

    def score_chunk(c, carry):
        for sb in range(sub):
            off = c * kc + sb * blk
            ikb = ik_ref[pl.ds(off, blk), :].astype(bf16)
            acc = jnp.zeros((blk, blk), f32)
            for hd in range(IDX_HEADS):
                s = jnp.dot(ikb, iqt_ref[hd * IDX_DIM:(hd + 1) * IDX_DIM, :], preferred_element_type=f32)
                acc = acc + jnp.maximum(s, 0.0) * iw[hd:hd + 1, :]
            causal = (row_b + off) <= col_b
            key = _sortable(jnp.where(causal, acc, -jnp.inf))
            key_ref[c, sb * blk:(sb + 1) * blk, :] = key
            hi_ref[c, sb * blk:(sb + 1) * blk, :] = _high_half(key)
        return carry

    chunks(score_chunk, 0)

    def count(pred):
        def body(c, acc):
            m = pred(key_ref[c], row_c + c * kc).astype(i32)
            return acc + jnp.sum(m.reshape(kc // SUBLANES, SUBLANES, blk), axis=0)
        c8 = chunks(body, jnp.zeros((SUBLANES, blk), i32))
        return jnp.sum(c8, axis=0, keepdims=True)

    def count16_ge(ref):
        packed = 2 * SUBLANES
        n_acc = 4

        def counter(cand):
            cand16 = cand.astype(i16)

            def body(c, accs):
                w = jnp.where(ref[c] >= cand16, jnp.int16(1), jnp.int16(0))
                accs = list(accs)
                for r in range(kc // packed):
                    accs[r % n_acc] = accs[r % n_acc] + w[r * packed:(r + 1) * packed]
                return tuple(accs)

            accs = chunks(body, tuple(jnp.zeros((packed, blk), i16) for _ in range(n_acc)))
            tot = (accs[0] + accs[1]) + (accs[2] + accs[3])
            return jnp.sum(tot.astype(i32), axis=0, keepdims=True)
        return counter

    def build_low(t_hi):
        for c in range(nkc):
            lo_ref[c] = _low_half(key_ref[c], t_hi)

    t = _kth_largest_packed(count16_ge(hi_ref), build_low, count16_ge(lo_ref), k_sel, (1, blk))

    count_ge = lambda cand: count(lambda kk, rows: kk >= cand)
    count_eq_before = lambda t, cand: count(lambda kk, rows: (kk == t) & (rows < cand))
    t, cut = _radix_select(t, count_ge, count_eq_before, k_sel, (1, blk), idx_bits)

    def bias_chunk(c, carry):
        for sb in range(sub):
            kk = key_ref[c, sb * blk:(sb + 1) * blk, :]
            mask = _select_mask(kk, row_b + (c * kc + sb * blk), t, cut)
            bias_ref[c, :, sb * blk:(sb + 1) * blk] = mask.T
        return carry

    chunks(bias_chunk, 0)

    def lane_fold(x, op):
        y = x[:, :blk]
        for sb in range(1, sub):
            y = op(y, x[:, sb * blk:(sb + 1) * blk])
        return y

    m_ref[...] = jnp.full(m_ref.shape, MASK_NEG, f32)

    def qk_chunk(c, carry):
        off = c * kc
        bias = bias_ref[c]
        for hd in range(ATT_HEADS):
            sl = slice(hd * HEAD_DIM, (hd + 1) * HEAD_DIM)
            kh = ak_ref[pl.ds(off, kc), sl]
            s = lax.dot_general(aq_ref[:, sl], kh, _NT, preferred_element_type=f32) * (HEAD_DIM ** -0.5) + bias
            s_ref[c, hd] = s
            m_ref[hd] = jnp.maximum(m_ref[hd], lane_fold(s, jnp.maximum))
        return carry

    chunks(qk_chunk, 0)
    m_fin = [jnp.max(m_ref[hd], axis=-1, keepdims=True) for hd in range(ATT_HEADS)]
    l_ref[...] = jnp.zeros(l_ref.shape, f32)
    acc_ref[...] = jnp.zeros(acc_ref.shape, f32)

    def pv_chunk(c, carry):
        off = c * kc
        for hd in range(ATT_HEADS):
            sl = slice(hd * HEAD_DIM, (hd + 1) * HEAD_DIM)
            p = jnp.exp(s_ref[c, hd] - m_fin[hd])
            l_ref[hd] = l_ref[hd] + lane_fold(p, jnp.add)
            acc_ref[hd] = acc_ref[hd] + jnp.dot(p.astype(bf16), av_ref[pl.ds(off, kc), sl], preferred_element_type=f32)
        return carry

    chunks(pv_chunk, 0)
    for hd in range(ATT_HEADS):
        o_ref[:, hd * HEAD_DIM:(hd + 1) * HEAD_DIM] = acc_ref[hd] / jnp.sum(l_ref[hd], axis=-1, keepdims=True)
    return 0


def _dsa_prompt(iqt, iwt, ik, aq, ak, av, n_batch):
    n = aq.shape[0]
    t = n // n_batch
    assert t % KEY_CHUNK == 0, t
    nqb = t // Q_BLOCK
    nkc = t // KEY_CHUNK
    k_sel = min(TOPK_MAX, t // 4)
    idx_bits = max(1, (t - 1).bit_length())
    qrow = lambda b, q: (b * nqb + q, 0)
    qcol = lambda b, q: (0, b * nqb + q)
    brow = lambda b, q: (b, 0)
    return pl.pallas_call(
        functools.partial(_dsa_prompt_kernel, k_sel=k_sel, idx_bits=idx_bits),
        grid=(n_batch, nqb),
        in_specs=[
            pl.BlockSpec((IDX_HEADS * IDX_DIM, Q_BLOCK), qcol),
            pl.BlockSpec((IDX_HEADS, Q_BLOCK), qcol),
            pl.BlockSpec((t, IDX_DIM), brow),
            pl.BlockSpec((Q_BLOCK, ATT_WIDTH), qrow),
            pl.BlockSpec((t, ATT_WIDTH), brow),
            pl.BlockSpec((t, ATT_WIDTH), brow),
        ],
        out_specs=pl.BlockSpec((Q_BLOCK, ATT_WIDTH), qrow),
        out_shape=jax.ShapeDtypeStruct((n, ATT_WIDTH), f32),
        scratch_shapes=[pltpu.VMEM((nkc, KEY_CHUNK, Q_BLOCK), i32), pltpu.VMEM((nkc, KEY_CHUNK, Q_BLOCK), i16),
                        pltpu.VMEM((nkc, KEY_CHUNK, Q_BLOCK), i16), pltpu.VMEM((nkc, Q_BLOCK, KEY_CHUNK), f32),
                        pltpu.VMEM((nkc, ATT_HEADS, Q_BLOCK, KEY_CHUNK), f32),
                        pltpu.VMEM((ATT_HEADS, Q_BLOCK, Q_BLOCK), f32),
                        pltpu.VMEM((ATT_HEADS, Q_BLOCK, Q_BLOCK), f32),
                        pltpu.VMEM((ATT_HEADS, Q_BLOCK, HEAD_DIM), f32)],
        compiler_params=_cparams(("arbitrary", "arbitrary")), name="dsa_prompt",
    )(iqt, iwt, ik, aq, ak, av)


def _page_scores(iq, iw, ik_page):
    s = jnp.dot(iq, ik_page.astype(bf16), preferred_element_type=f32)
    s = jnp.maximum(s * (IDX_DIM ** -0.5), 0.0) * iw
    return jnp.sum(s.reshape(IDX_HEADS, SAMPLE_Q, PAGE_SIZE), axis=0)


def _idx_scores_kernel(pt_ref, iq_ref, iw_ref, *refs, pages):
    page_refs, o_ref = refs[:pages], refs[pages]
    iq, iw = iq_ref[...], iw_ref[...]
    for i in range(pages):
        o_ref[i] = _page_scores(iq, iw, page_refs[i][...])


def _idx_scores(page_table, iq_s, iw_s, kidx_t, layer, pages):
    db, n_pages = page_table.shape
    nj = n_pages // pages
    page_spec = lambda i: pl.BlockSpec((None, None, IDX_DIM, PAGE_SIZE),
                                       lambda b, j, pt: (layer, pt[b, j * pages + i], 0, 0))
    return pl.pallas_call(
        functools.partial(_idx_scores_kernel, pages=pages),
        grid_spec=pltpu.PrefetchScalarGridSpec(
            num_scalar_prefetch=1, grid=(db, nj),
            in_specs=[pl.BlockSpec((None, IDX_HEADS * SAMPLE_Q, IDX_DIM), lambda b, j, pt: (b, 0, 0)),
                      pl.BlockSpec((None, IDX_HEADS * SAMPLE_Q, LANES), lambda b, j, pt: (b, 0, 0))]
                     + [page_spec(i) for i in range(pages)],
            out_specs=pl.BlockSpec((None, pages, SAMPLE_Q, PAGE_SIZE), lambda b, j, pt: (b, j, 0, 0)),
        ),
        out_shape=jax.ShapeDtypeStruct((db, n_pages, SAMPLE_Q, PAGE_SIZE), f32),
        compiler_params=_cparams(("arbitrary", "arbitrary")), name="idx_scores",
    )(page_table, iq_s, iw_s, *([kidx_t] * pages))


def _select_kernel(sc_ref, iq_ref, iw_ref, ikn_ref, bias_ref, key_ref, hi_ref, lo_ref, *, k_sel, idx_bits):
    n_req, n_pages = sc_ref.shape[0], sc_ref.shape[1]
    rows = n_req * SAMPLE_Q
    nch = n_pages + 1
    unroll = 8 if n_pages % 8 == 0 else 1
    lane_i = lax.broadcasted_iota(i32, (rows, LANES), 1)
    q_i = lax.broadcasted_iota(i32, (SAMPLE_Q, LANES), 0)
    l_i = lax.broadcasted_iota(i32, (SAMPLE_Q, LANES), 1)

    def to_key(ch, carry):
        key = _sortable(jnp.concatenate([sc_ref[r, ch] for r in range(n_req)], axis=0))
        key_ref[ch] = key
        hi_ref[ch] = _high_half(key)
        return carry

    lax.fori_loop(0, n_pages, to_key, 0)
    new = []
    for r in range(n_req):
        s = _page_scores(iq_ref[r], iw_ref[r], ikn_ref[r])
        new.append(jnp.where(l_i <= q_i, s, -jnp.inf))
    key_new = _sortable(jnp.concatenate(new, axis=0))
    key_ref[n_pages] = key_new
    hi_ref[n_pages] = _high_half(key_new)

    def count16_ge(ref):
        def counter(cand):
            cand16 = jnp.broadcast_to(cand, (rows, LANES)).astype(i16)
            hit = lambda ch: jnp.where(ref[ch] >= cand16, jnp.int16(1), jnp.int16(0))

            def body(g, accs):
                accs = list(accs)
                for u in range(unroll):
                    accs[u % 2] = accs[u % 2] + hit(g * unroll + u)
                return tuple(accs)

            zero = jnp.zeros((rows, LANES), i16)
            a0, a1 = lax.fori_loop(0, n_pages // unroll, body, (zero, zero))
            tot = (a0 + a1) + hit(n_pages)
            return jnp.sum(tot.astype(i32), axis=-1, keepdims=True)
        return counter

    def build_low(t_hi):
        t_hi_b = jnp.broadcast_to(t_hi, (rows, LANES))

        def body(ch, carry):
            lo_ref[ch] = _low_half(key_ref[ch], t_hi_b)
            return carry
        lax.fori_loop(0, nch, body, 0)

    def count_ge(cand):
        cand_b = jnp.broadcast_to(cand, (rows, LANES))

        def body(ch, c):
            return c + (key_ref[ch] >= cand_b).astype(i32)
        c = lax.fori_loop(0, nch, body, jnp.zeros((rows, LANES), i32))
        return jnp.sum(c, axis=-1, keepdims=True)

    def count_eq_before(t, cand):
        t_b = jnp.broadcast_to(t, (rows, LANES))
        cand_b = jnp.broadcast_to(cand, (rows, LANES))

        def body(ch, c):
            return c + ((key_ref[ch] == t_b) & ((lane_i + ch * LANES) < cand_b)).astype(i32)
        c = lax.fori_loop(0, nch, body, jnp.zeros((rows, LANES), i32))
        return jnp.sum(c, axis=-1, keepdims=True)

    t = _kth_largest_packed(count16_ge(hi_ref), build_low, count16_ge(lo_ref), k_sel, (rows, 1))
    t, c = _radix_select(t, count_ge, count_eq_before, k_sel, (rows, 1), idx_bits)
    t_b = jnp.broadcast_to(t, (rows, LANES))
    c_b = jnp.broadcast_to(c, (rows, LANES))

    def to_bias(ch, carry):
        bias = _select_mask(key_ref[ch], lane_i + ch * LANES, t_b, c_b)
        for r in range(n_req):
            bias_ref[r, ch] = bias[r * SAMPLE_Q:(r + 1) * SAMPLE_Q]
        return carry

    lax.fori_loop(0, nch, to_bias, 0)


def _select(scores, iq_s, iw_s, ik_new, k_sel, n_req):
    db, n_pages = scores.shape[0], scores.shape[1]
    nch = n_pages + 1
    idx_bits = max(1, (nch * LANES - 1).bit_length())
    req3 = lambda g: (g, 0, 0)
    req4 = lambda g: (g, 0, 0, 0)
    return pl.pallas_call(
        functools.partial(_select_kernel, k_sel=k_sel, idx_bits=idx_bits),
        grid=(db // n_req,),
        in_specs=[pl.BlockSpec((n_req, n_pages, SAMPLE_Q, LANES), req4),
                  pl.BlockSpec((n_req, IDX_HEADS * SAMPLE_Q, IDX_DIM), req3),
                  pl.BlockSpec((n_req, IDX_HEADS * SAMPLE_Q, LANES), req3),
                  pl.BlockSpec((n_req, IDX_DIM, PAGE_SIZE), req3)],
        out_specs=pl.BlockSpec((n_req, nch, SAMPLE_Q, LANES), req4),
        out_shape=jax.ShapeDtypeStruct((db, nch, SAMPLE_Q, LANES), f32),
        scratch_shapes=[pltpu.VMEM((nch, n_req * SAMPLE_Q, LANES), i32),
                        pltpu.VMEM((nch, n_req * SAMPLE_Q, LANES), i16),
                        pltpu.VMEM((nch, n_req * SAMPLE_Q, LANES), i16)],
        compiler_params=_cparams(("arbitrary",)), name="select",
    )(scores, iq_s, iw_s, ik_new)


def _paged_attn_kernel(pt_ref, q_ref, hm_ref, ex_ref, bias_ref, biasn_ref, kn_ref, vn_ref, *refs, pages):
    k_refs, v_refs = refs[:pages], refs[pages:2 * pages]
    o_ref, q_s, m_ref, l_ref, acc_ref = refs[2 * pages:]
    j = pl.program_id(1)
    rows = ATT_HEADS * SAMPLE_Q

    @pl.when(j == 0)
    def _():
        qf = q_ref[...].astype(f32)
        q_s[...] = jnp.concatenate(
            [qf[:SAMPLE_Q, hd * HEAD_DIM:(hd + 1) * HEAD_DIM] for hd in range(ATT_HEADS)], axis=0).astype(bf16)
        m_ref[...] = jnp.full(m_ref.shape, MASK_NEG, f32)
        l_ref[...] = jnp.zeros(l_ref.shape, f32)
        acc_ref[...] = jnp.zeros(acc_ref.shape, f32)

    q = q_s[...]
    hm = hm_ref[...]

    def scores(k_page, b8):
        s = lax.dot_general(q, k_page.astype(bf16), _NT, preferred_element_type=f32) * (HEAD_DIM ** -0.5)
        sel = jnp.where(b8 == 0.0, 1.0, 0.0)
        sel = jnp.concatenate([sel] * ATT_HEADS, axis=0).astype(bf16)
        selx = jnp.dot(sel, ex_ref[...], preferred_element_type=f32) * hm
        return jnp.where(selx > 0.5, s, MASK_NEG)

    def accumulate(s_list, v_list):
        m_old = m_ref[...]
        mx = s_list[0]
        for s in s_list[1:]:
            mx = jnp.maximum(mx, s)
        m_new = jnp.maximum(m_old, jnp.max(mx, axis=-1, keepdims=True))
        alpha = jnp.exp(m_old - m_new)
        psum = jnp.zeros((rows, ATT_HEADS * PAGE_SIZE), f32)
        pv = jnp.zeros((rows, HEAD_DIM), f32)
        for s, v in zip(s_list, v_list):
            p = jnp.exp(s - m_new)
            psum = psum + p
            pv = pv + jnp.dot(p.astype(bf16), v.astype(bf16), preferred_element_type=f32)
        l_ref[...] = alpha * l_ref[...] + jnp.sum(psum, axis=-1, keepdims=True)
        acc_ref[...] = alpha * acc_ref[...] + pv
        m_ref[...] = m_new

    accumulate([scores(k_refs[i][...], bias_ref[i]) for i in range(pages)],
               [v_refs[i][...] for i in range(pages)])

    @pl.when(j == pl.num_programs(1) - 1)
    def _():
        accumulate([scores(kn_ref[...], biasn_ref[...])], [vn_ref[...]])
        o = acc_ref[...] / l_ref[...]
        o_ref[...] = jnp.zeros(o_ref.shape, f32)
        for hd in range(ATT_HEADS):
            o_ref[:SAMPLE_Q, hd * HEAD_DIM:(hd + 1) * HEAD_DIM] = o[hd * SAMPLE_Q:(hd + 1) * SAMPLE_Q]


def _paged_attn(page_table, aq_s, bias, k_new, v_new, cache_k, cache_v, layer, pages):
    db, n_pages = page_table.shape
    nj = n_pages // pages
    page_rows = PAGE_SIZE * ATT_HEADS
    rows = ATT_HEADS * SAMPLE_Q
    col = jnp.arange(page_rows)
    head_match = (col[None, :] % ATT_HEADS == jnp.arange(rows)[:, None] // SAMPLE_Q).astype(f32)
    expand = (col[None, :] // ATT_HEADS == jnp.arange(PAGE_SIZE)[:, None]).astype(bf16)
    page_spec = lambda i: pl.BlockSpec((None, None, page_rows, HEAD_DIM),
                                       lambda b, j, pt: (layer, pt[b, j * pages + i], 0, 0))
    req = lambda b, j, pt: (b, 0)
    const = lambda b, j, pt: (0, 0)
    return pl.pallas_call(
        functools.partial(_paged_attn_kernel, pages=pages),
        grid_spec=pltpu.PrefetchScalarGridSpec(
            num_scalar_prefetch=1, grid=(db, nj),
            in_specs=[pl.BlockSpec((SAMPLE_ROWS, ATT_WIDTH), req),
                      pl.BlockSpec(head_match.shape, const),
                      pl.BlockSpec(expand.shape, const),
                      pl.BlockSpec((None, pages, SAMPLE_Q, LANES), lambda b, j, pt: (b, j, 0, 0)),
                      pl.BlockSpec((None, None, SAMPLE_Q, LANES), lambda b, j, pt: (b, n_pages, 0, 0)),
                      pl.BlockSpec((page_rows, HEAD_DIM), req),
                      pl.BlockSpec((page_rows, HEAD_DIM), req)]
                     + [page_spec(i) for i in range(pages)] * 2,
            out_specs=pl.BlockSpec((SAMPLE_ROWS, ATT_WIDTH), req),
            scratch_shapes=[pltpu.VMEM((rows, HEAD_DIM), bf16),
                            pltpu.VMEM((rows, 1), f32),
                            pltpu.VMEM((rows, 1), f32),
                            pltpu.VMEM((rows, HEAD_DIM), f32)],
        ),
        out_shape=jax.ShapeDtypeStruct((db * SAMPLE_ROWS, ATT_WIDTH), f32),
        compiler_params=_cparams(("arbitrary", "arbitrary")), name="paged_attn",
    )(page_table, aq_s, head_match, expand, bias, bias, k_new, v_new, *([cache_k] * pages), *([cache_v] * pages))


def _merge_kernel(x_ref, oret_ref, oatt_ref, ng_ref, wg_ref, wbr_ref, wba_ref, wo_ref, y_ref):
    x = x_ref[...]
    ms = jnp.mean(x * x, axis=-1, keepdims=True)
    h = ((x * lax.rsqrt(ms + EPS)) * ng_ref[...]).astype(bf16)
    g = jnp.dot(h, wg_ref[...], preferred_element_type=f32)
    rg = g[:, 0:RET_WIDTH]
    ag = g[:, RET_WIDTH:RET_WIDTH + ATT_WIDTH]
    ga = g[:, RET_WIDTH + ATT_WIDTH:RET_WIDTH + ATT_WIDTH + D_MODEL]
    gb = g[:, RET_WIDTH + ATT_WIDTH + D_MODEL:]
    o_ret = oret_ref[...] * (rg * jax.nn.sigmoid(rg))
    o_att = oatt_ref[...] * (ag * jax.nn.sigmoid(ag))
    a = jnp.dot(o_ret.astype(bf16), wbr_ref[...], preferred_element_type=f32)
    b = jnp.dot(o_att.astype(bf16), wba_ref[...], preferred_element_type=f32)
    y = jax.nn.sigmoid(ga) * a + jax.nn.sigmoid(gb) * b
    y_ref[...] = x + jnp.dot(y.astype(bf16), wo_ref[...], preferred_element_type=f32)


def _merge(x2d, o_ret, o_att, w, tm):
    n = x2d.shape[0]
    row = lambda i: (i, 0)
    full = lambda a: pl.BlockSpec(a.shape, lambda i: (0, 0))
    return pl.pallas_call(
        _merge_kernel, grid=(n // tm,),
        in_specs=[pl.BlockSpec((tm, D_MODEL), row), pl.BlockSpec((tm, RET_WIDTH), row),
                  pl.BlockSpec((tm, ATT_WIDTH), row),
                  full(w["ng"]), full(w["wg"]), full(w["wbr"]), full(w["wba"]), full(w["wo"])],
        out_specs=pl.BlockSpec((tm, D_MODEL), row),
        out_shape=jax.ShapeDtypeStruct((n, D_MODEL), f32),
        compiler_params=_cparams(("arbitrary",)), name="merge",
    )(x2d, o_ret, o_att, w["ng"], w["wg"], w["wbr"], w["wba"], w["wo"])


def _prep_weights(l, norm_g, w_in, q_norm_g, k_norm_g, kidx_norm_g, w_br_ret, w_br_att, w_out):
    w = w_in[l]
    pad_cols = lambda a, n: jnp.pad(a, ((0, 0), (0, n - a.shape[1])))
    wiq = jnp.concatenate([w[:, _O_IQ:_O_IK], w[:, _O_IW:_O_GA]], axis=1).T
    wiq = jnp.pad(wiq, ((0, 2 * SUBLANES - IDX_HEADS), (0, 0)))
    return {
        "ng": norm_g[l].reshape(1, D_MODEL),
        "wr": w[:, _O_RQ:_O_RG].astype(bf16),
        "wa": jnp.concatenate([w[:, _O_AQ:_O_AG]], axis=1).astype(bf16),
        "wiq": wiq.astype(bf16),
        "wik": pad_cols(w[:, _O_IK:_O_IW], LANES).astype(bf16),
        "qg": q_norm_g[l].reshape(1, HEAD_DIM),
        "kg": k_norm_g[l].reshape(1, HEAD_DIM),
        "ig": pad_cols(kidx_norm_g[l].reshape(1, IDX_DIM), LANES),
        "wg": jnp.concatenate([w[:, _O_RG:_O_AQ], w[:, _O_AG:_O_IQ], w[:, _O_GA:_O_END]], axis=1).astype(bf16),
        "wbr": w_br_ret[l].astype(bf16),
        "wba": w_br_att[l].astype(bf16),
        "wo": w_out[l].astype(bf16),
    }


def _position_tables(pos):
    rc, rs, _, _ = _rope_tables(pos, RET_DK, RET_ROPE_THETA, LANES)
    ac, as_, _, _ = _rope_tables(pos, ROPE_DIM, ROPE_THETA, LANES)
    ic, is_, ict, ist = _rope_tables(pos, IDX_ROPE_DIM, ROPE_THETA, LANES)
    return rc, rs, ac, as_, ic, is_, ict, ist


def _token_tile(n):
    for tm in (256, 128):
        if n % tm == 0:
            return tm
    assert n % SAMPLE_ROWS == 0 and n < LANES, n
    return n


def _prompt_layer(x2d, n_batch, w, ret_g, tabs, tm):
    rq, rk, rv, aq, ak, av, akb, avb, iqt, ik, iwt = _inproj(x2d, w, tabs, tm)
    o_ret, s_new = _retention(rq, rk, rv, ret_g, n_batch, RET_CHUNK)
    o_att = _dsa_prompt(iqt, iwt, ik, aq, akb, avb, n_batch)
    y = _merge(x2d, o_ret, o_att, w, tm)
    return y, ak, av, ik, s_new


def _sample_layer(xs, n_true, w, ret_g, tabs, tm, layer, state_ret, page_table, cache_k, cache_v, kidx_t, pages):
    db = xs.shape[0]
    n = db * SAMPLE_ROWS
    past = page_table.shape[1] * PAGE_SIZE
    x2d = xs.reshape(n, D_MODEL)
    rq, rk, rv, aq, ak, av, _, _, iqt, ik, iwt = _inproj(x2d, w, tabs, tm)

    padc = lambda a: jnp.pad(a.reshape(db, SAMPLE_ROWS, -1), ((0, 0), (0, RET_CHUNK - SAMPLE_ROWS), (0, 0))
                             ).reshape(db * RET_CHUNK, -1)
    o_ret, s_new = _retention(padc(rq), padc(rk), padc(rv), ret_g, db, n_true, state=state_ret, layer=layer)
    o_ret = o_ret.reshape(db, RET_CHUNK, RET_WIDTH)[:, :SAMPLE_ROWS].reshape(n, RET_WIDTH)

    iq_s = iqt.reshape(IDX_HEADS, IDX_DIM, db, SAMPLE_ROWS)[:, :, :, :SAMPLE_Q]
    iq_s = iq_s.transpose(2, 0, 3, 1).reshape(db, IDX_HEADS * SAMPLE_Q, IDX_DIM)
    iw_s = iwt.reshape(IDX_HEADS, db, SAMPLE_ROWS)[:, :, :SAMPLE_Q].transpose(1, 0, 2)
    iw_s = jnp.broadcast_to(iw_s.reshape(db, IDX_HEADS * SAMPLE_Q, 1), (db, IDX_HEADS * SAMPLE_Q, LANES))
    ik_new = jnp.pad(ik.reshape(db, SAMPLE_ROWS, IDX_DIM), ((0, 0), (0, PAGE_SIZE - SAMPLE_ROWS), (0, 0)))
    ik_new = ik_new.transpose(0, 2, 1)

    scores = _idx_scores(page_table, iq_s, iw_s, kidx_t, layer, pages)
    k_sel = min(TOPK_MAX, (past + n_true) // 4)
    n_req = 8 if db % 8 == 0 else db
    bias = _select(scores, iq_s, iw_s, ik_new, k_sel, n_req)

    pad_new = lambda a: jnp.pad(a.reshape(db, SAMPLE_ROWS * ATT_HEADS, HEAD_DIM),
                                ((0, 0), (0, (PAGE_SIZE - SAMPLE_ROWS) * ATT_HEADS), (0, 0))
                                ).reshape(db * PAGE_SIZE * ATT_HEADS, HEAD_DIM)
    o_att = _paged_attn(page_table, aq, bias, pad_new(ak), pad_new(av), cache_k, cache_v, layer, pages)
    y = _merge(x2d, o_ret, o_att, w, tm)
    return y.reshape(db, SAMPLE_ROWS, D_MODEL), ak, av, ik, s_new


def kernel(x_prompt, x_sample, cache_k, cache_v, cache_kidx, state_ret, page_table, norm_g, w_in, q_norm_g,
           k_norm_g, kidx_norm_g, ret_norm_g, w_br_ret, w_br_att, w_out):
    depth = w_in.shape[0]
    n_batch, t, _ = x_prompt.shape
    db, ds, _ = x_sample.shape
    n_pages = page_table.shape[1]
    past = n_pages * PAGE_SIZE
    assert t % RET_CHUNK == 0 and ds <= SAMPLE_Q

    tm_p = _token_tile(t)
    tabs_p = _position_tables(jnp.arange(t))
    n_s = db * SAMPLE_ROWS
    tm_s = _token_tile(n_s)
    pos_s = past + (jnp.arange(tm_s) % SAMPLE_ROWS)
    tabs_s = _position_tables(pos_s)
    pages = 16 if n_pages % 16 == 0 else n_pages

    n_pool = cache_k.shape[1]
    cache_k = cache_k.reshape(depth, n_pool, PAGE_SIZE * ATT_HEADS, HEAD_DIM)
    cache_v = cache_v.reshape(depth, n_pool, PAGE_SIZE * ATT_HEADS, HEAD_DIM)
    kidx_t = jnp.swapaxes(cache_kidx, 2, 3)

    xp = x_prompt.reshape(n_batch * t, D_MODEL)
    pad_s = lambda a: jnp.pad(a, ((0, 0), (0, SAMPLE_ROWS - ds), (0, 0)))
    xs = pad_s(x_sample)
    kp, vp, ip, sp, kd, vd, idd, sd = [], [], [], [], [], [], [], []
    for l in range(depth):
        w = _prep_weights(l, norm_g, w_in, q_norm_g, k_norm_g, kidx_norm_g, w_br_ret, w_br_att, w_out)
        xp, k_new, v_new, i_new, s_new = _prompt_layer(xp, n_batch, w, ret_norm_g[l], tabs_p, tm_p)
        kp.append(k_new.reshape(n_batch, t, ATT_HEADS, HEAD_DIM))
        vp.append(v_new.reshape(n_batch, t, ATT_HEADS, HEAD_DIM))
        ip.append(i_new.reshape(n_batch, t, IDX_DIM))
        sp.append(s_new)
        ys, k_new, v_new, i_new, s_new = _sample_layer(
            xs, ds, w, ret_norm_g[l], tabs_s, tm_s, l, state_ret, page_table, cache_k, cache_v, kidx_t, pages)
        xs = pad_s(ys[:, :ds])
        kd.append(k_new.reshape(db, SAMPLE_ROWS, ATT_HEADS, HEAD_DIM)[:, :ds])
        vd.append(v_new.reshape(db, SAMPLE_ROWS, ATT_HEADS, HEAD_DIM)[:, :ds])
        idd.append(i_new.reshape(db, SAMPLE_ROWS, IDX_DIM)[:, :ds])
        sd.append(s_new)
    return (xp.reshape(n_batch, t, D_MODEL), xs[:, :ds],
            jnp.stack(kp), jnp.stack(vp), jnp.stack(ip), jnp.stack(sp),
            jnp.stack(kd), jnp.stack(vd), jnp.stack(idd), jnp.stack(sd))
```

```python
import functools
import math

import jax
import jax.numpy as jnp
from jax import lax
from jax.experimental import pallas as pl
from jax.experimental.pallas import tpu as pltpu

D_MODEL = 1024
PAGE_SIZE = 128
RET_HEADS = 4
RET_DK = 128
RET_DV = 128
RET_WIDTH = RET_HEADS * RET_DV
RET_CHUNK = 128
RET_ROPE_THETA = 10000.0
ATT_HEADS = 4
HEAD_DIM = 128
ATT_WIDTH = ATT_HEADS * HEAD_DIM
ROPE_DIM = HEAD_DIM // 4
ROPE_THETA = 500000.0
IDX_HEADS = 8
IDX_DIM = 64
IDX_ROPE_DIM = IDX_DIM // 4
TOPK_MAX = 256
Q_BLOCK = 128
EPS = 1e-6

LANES = 128
SUBLANES = 8
VMEM_LIMIT = 56 * 1024 * 1024

_O_RQ, _O_RK, _O_RV, _O_RG = 0, 512, 1024, 1536
_O_AQ, _O_AK, _O_AV, _O_AG = 2048, 2560, 3072, 3584
_O_IQ, _O_IK, _O_IW = 4096, 4608, 4672
_O_GA, _O_GB, _O_END = 4680, 5704, 6728

KEY_CHUNK = 512
SAMPLE_ROWS = 16
SAMPLE_Q = 8
MASK_NEG = -1e30

f32 = jnp.float32
bf16 = jnp.bfloat16
i32 = jnp.int32
i16 = jnp.int16

_NT = (((1,), (1,)), ((), ()))


def _cparams(sem, flags=None):
    return pltpu.CompilerParams(dimension_semantics=sem, vmem_limit_bytes=VMEM_LIMIT, flags=flags)


def _sortable(s):
    b = pltpu.bitcast(s + 0.0, i32)
    return b ^ ((b >> 31) & jnp.int32(0x7FFFFFFF))


_NEG_INF_KEY = -2139095041


def _rope_tables(pos, rot_dim, theta, width):
    half = rot_dim // 2
    freqs = jnp.exp(-jnp.arange(half, dtype=f32) * (2.0 / rot_dim) * math.log(theta))
    ang = pos.astype(f32)[:, None] * freqs[None, :]
    cos, sin = jnp.cos(ang), jnp.sin(ang)
    t = pos.shape[0]
    c = jnp.concatenate([cos, cos, jnp.ones((t, width - rot_dim), f32)], axis=-1)
    s = jnp.concatenate([sin, -sin, jnp.zeros((t, width - rot_dim), f32)], axis=-1)
    return c, s, cos.T, sin.T


def _rope_lanes(x, c, s, half, rot, lane):
    w = x * s
    if rot == LANES:
        return x * c + pltpu.roll(w, half, 1)
    up = pltpu.roll(w, LANES - half, 1)
    dn = pltpu.roll(w, half, 1)
    return x * c + jnp.where(lane < half, up, jnp.where(lane < rot, dn, 0.0))


def _inproj_kernel(x_ref, ng_ref, wr_ref, wa_ref, wiq_ref, wik_ref, qg_ref, kg_ref, ig_ref,
                   rc_ref, rs_ref, ac_ref, as_ref, ic_ref, is_ref, ict_ref, ist_ref,
                   rq_ref, rk_ref, rv_ref, aq_ref, ak_ref, av_ref, akb_ref, avb_ref, iqt_ref, ik_ref, iwt_ref):
    x = x_ref[...]
    tm = x.shape[0]
    ms = jnp.mean(x * x, axis=-1, keepdims=True)
    h = ((x * lax.rsqrt(ms + EPS)) * ng_ref[...]).astype(bf16)
    lane = lax.broadcasted_iota(i32, (tm, LANES), 1)

    r = jnp.dot(h, wr_ref[...], preferred_element_type=f32)
    rc, rs = rc_ref[...], rs_ref[...]
    for hd in range(RET_HEADS):
        sl = slice(hd * RET_DK, (hd + 1) * RET_DK)
        q = r[:, _O_RQ + hd * RET_DK:_O_RQ + (hd + 1) * RET_DK]
        k = r[:, _O_RK + hd * RET_DK:_O_RK + (hd + 1) * RET_DK]
        rq_ref[:, sl] = _rope_lanes(q, rc, rs, RET_DK // 2, RET_DK, lane).astype(bf16)
        rk_ref[:, sl] = (_rope_lanes(k, rc, rs, RET_DK // 2, RET_DK, lane) * (RET_DK ** -0.5)).astype(bf16)
    rv_ref[...] = r[:, _O_RV:_O_RV + RET_WIDTH].astype(bf16)

    a = jnp.dot(h, wa_ref[...], preferred_element_type=f32)
    ac, as_ = ac_ref[...], as_ref[...]
    k_heads = []
    for hd in range(ATT_HEADS):
        sl = slice(hd * HEAD_DIM, (hd + 1) * HEAD_DIM)
        q = a[:, hd * HEAD_DIM:(hd + 1) * HEAD_DIM]
        k = a[:, ATT_WIDTH + hd * HEAD_DIM:ATT_WIDTH + (hd + 1) * HEAD_DIM]
        q = (q * lax.rsqrt(jnp.mean(q * q, axis=-1, keepdims=True) + EPS)) * qg_ref[...]
        k = (k * lax.rsqrt(jnp.mean(k * k, axis=-1, keepdims=True) + EPS)) * kg_ref[...]
        aq_ref[:, sl] = _rope_lanes(q, ac, as_, ROPE_DIM // 2, ROPE_DIM, lane).astype(bf16)
        k = _rope_lanes(k, ac, as_, ROPE_DIM // 2, ROPE_DIM, lane)
        k_heads.append(k)
        akb_ref[:, sl] = k.astype(bf16)
    v = a[:, 2 * ATT_WIDTH:3 * ATT_WIDTH]
    ak_ref[...] = jnp.concatenate(k_heads, axis=-1).reshape(tm, ATT_HEADS, HEAD_DIM)
    av_ref[...] = v.reshape(tm, ATT_HEADS, HEAD_DIM)
    avb_ref[...] = v.astype(bf16)

    pk = jnp.dot(h, wik_ref[...], preferred_element_type=f32)
    mk = jnp.sum(pk * pk, axis=-1, keepdims=True) * (1.0 / IDX_DIM)
    pk = (pk * lax.rsqrt(mk + EPS)) * ig_ref[...]
    pk = _rope_lanes(pk, ic_ref[...], is_ref[...], IDX_ROPE_DIM // 2, IDX_ROPE_DIM, lane)
    ik_ref[...] = pk[:, :IDX_DIM]

    t = lax.dot_general(wiq_ref[...], h, _NT, preferred_element_type=f32)
    ct, st = ict_ref[...], ist_ref[...]
    hr = IDX_ROPE_DIM // 2
    for hd in range(IDX_HEADS):
        base = hd * IDX_DIM
        x1 = t[base:base + hr]
        x2 = t[base + hr:base + 2 * hr]
        blk = jnp.concatenate([x1 * ct - x2 * st, x1 * st + x2 * ct, t[base + 2 * hr:base + IDX_DIM]], axis=0)
        iqt_ref[base:base + IDX_DIM, :] = blk.astype(bf16)
    iwt_ref[...] = t[IDX_HEADS * IDX_DIM:IDX_HEADS * IDX_DIM + IDX_HEADS] * (IDX_HEADS ** -0.5)


def _inproj(x2d, w, tabs, tm):
    n = x2d.shape[0]
    rc, rs, ac, as_, ic, is_, ict, ist = tabs
    tt = rc.shape[0]
    nj, nr = tt // tm, n // tt
    row = lambda j, r: (r * nj + j, 0)
    row3 = lambda j, r: (r * nj + j, 0, 0)
    col = lambda j, r: (0, r * nj + j)
    tab = lambda j, r: (j, 0)
    tabt = lambda j, r: (0, j)
    const = lambda j, r: (0, 0)
    full = lambda a: pl.BlockSpec(a.shape, const)
    in_specs = [
        pl.BlockSpec((tm, D_MODEL), row),
        full(w["ng"]), full(w["wr"]), full(w["wa"]), full(w["wiq"]), full(w["wik"]),
        full(w["qg"]), full(w["kg"]), full(w["ig"]),
        pl.BlockSpec((tm, LANES), tab), pl.BlockSpec((tm, LANES), tab),
        pl.BlockSpec((tm, LANES), tab), pl.BlockSpec((tm, LANES), tab),
        pl.BlockSpec((tm, LANES), tab), pl.BlockSpec((tm, LANES), tab),
        pl.BlockSpec((IDX_ROPE_DIM // 2, tm), tabt), pl.BlockSpec((IDX_ROPE_DIM // 2, tm), tabt),
    ]
    out_shape = [
        jax.ShapeDtypeStruct((n, RET_WIDTH), bf16), jax.ShapeDtypeStruct((n, RET_WIDTH), bf16),
        jax.ShapeDtypeStruct((n, RET_WIDTH), bf16), jax.ShapeDtypeStruct((n, ATT_WIDTH), bf16),
        jax.ShapeDtypeStruct((n, ATT_HEADS, HEAD_DIM), f32), jax.ShapeDtypeStruct((n, ATT_HEADS, HEAD_DIM), f32),
        jax.ShapeDtypeStruct((n, ATT_WIDTH), bf16), jax.ShapeDtypeStruct((n, ATT_WIDTH), bf16),
        jax.ShapeDtypeStruct((IDX_HEADS * IDX_DIM, n), bf16),
        jax.ShapeDtypeStruct((n, IDX_DIM), f32),
        jax.ShapeDtypeStruct((IDX_HEADS, n), f32),
    ]
    out_specs = [
        pl.BlockSpec((tm, RET_WIDTH), row), pl.BlockSpec((tm, RET_WIDTH), row),
        pl.BlockSpec((tm, RET_WIDTH), row), pl.BlockSpec((tm, ATT_WIDTH), row),
        pl.BlockSpec((tm, ATT_HEADS, HEAD_DIM), row3), pl.BlockSpec((tm, ATT_HEADS, HEAD_DIM), row3),
        pl.BlockSpec((tm, ATT_WIDTH), row), pl.BlockSpec((tm, ATT_WIDTH), row),
        pl.BlockSpec((IDX_HEADS * IDX_DIM, tm), col),
        pl.BlockSpec((tm, IDX_DIM), row),
        pl.BlockSpec((IDX_HEADS, tm), col),
    ]
    return pl.pallas_call(
        _inproj_kernel, grid=(nj, nr), in_specs=in_specs, out_specs=out_specs, out_shape=out_shape,
        compiler_params=_cparams(("arbitrary", "arbitrary")), name="inproj",
    )(x2d, w["ng"], w["wr"], w["wa"], w["wiq"], w["wik"], w["qg"], w["kg"], w["ig"],
      rc, rs, ac, as_, ic, is_, ict, ist)


def _retention_kernel(*refs, has_state):
    if has_state:
        rq_ref, rk_ref, rv_ref, dm_ref, qd_ref, kd_ref, cd_ref, g_ref, s0_ref, o_ref, st_ref = refs
    else:
        rq_ref, rk_ref, rv_ref, dm_ref, qd_ref, kd_ref, cd_ref, g_ref, o_ref, st_ref = refs

    @pl.when(pl.program_id(1) == 0)
    def _():
        if has_state:
            st_ref[...] = s0_ref[...]
        else:
            st_ref[...] = jnp.zeros(st_ref.shape, f32)

    for hd in range(RET_HEADS):
        sl = slice(hd * RET_DK, (hd + 1) * RET_DK)
        q, k, v = rq_ref[:, sl], rk_ref[:, sl], rv_ref[:, sl]
        st = st_ref[hd]
        s = lax.dot_general(q, k, _NT, preferred_element_type=f32) * dm_ref[hd]
        inner = jnp.dot(s.astype(bf16), v, preferred_element_type=f32)
        qd = (q.astype(f32) * qd_ref[hd]).astype(bf16)
        cross = jnp.dot(qd, st.astype(bf16), preferred_element_type=f32)
        kdt = (k.astype(f32) * kd_ref[hd]).T.astype(bf16)
        st_ref[hd] = cd_ref[hd] * st + jnp.dot(kdt, v, preferred_element_type=f32)
        o = inner + cross
        o = (o * lax.rsqrt(jnp.mean(o * o, axis=-1, keepdims=True) + EPS)) * g_ref[hd]
        o_ref[:, sl] = o


def _ret_log_decay():
    return jnp.log1p(-jnp.exp2(-5.0 - jnp.arange(RET_HEADS, dtype=f32)))


def _retention_tables(c_block, c_true):
    log_g = _ret_log_decay()
    i = jnp.arange(c_block, dtype=f32)
    diff = i[:, None] - i[None, :]
    dmask = jnp.where(diff >= 0, jnp.exp(log_g[:, None, None] * jnp.maximum(diff, 0.0)), 0.0)
    q_dec = jnp.exp(log_g[:, None] * (i + 1.0)[None, :])
    k_dec = jnp.where(i[None, :] < c_true, jnp.exp(log_g[:, None] * (c_true - 1.0 - i)[None, :]), 0.0)
    chunk_decay = jnp.exp(log_g * c_true)
    bc = lambda a: jnp.broadcast_to(a[:, :, None], (RET_HEADS, c_block, LANES))
    cd = jnp.broadcast_to(chunk_decay[:, None, None], (RET_HEADS, 1, LANES))
    return dmask, bc(q_dec), bc(k_dec), cd


def _retention(rq, rk, rv, ret_g, n_batch, c_true, state=None, layer=0):
    n = rq.shape[0]
    c = RET_CHUNK
    nc = n // n_batch // c
    dm, qd, kd, cd = _retention_tables(c, c_true)
    g = ret_g.reshape(RET_HEADS, 1, RET_DV)
    row = lambda b, j: (b * nc + j, 0)
    c3 = lambda b, j: (0, 0, 0)
    in_specs = [
        pl.BlockSpec((c, RET_WIDTH), row), pl.BlockSpec((c, RET_WIDTH), row), pl.BlockSpec((c, RET_WIDTH), row),
        pl.BlockSpec(dm.shape, c3), pl.BlockSpec(qd.shape, c3), pl.BlockSpec(kd.shape, c3),
        pl.BlockSpec(cd.shape, c3), pl.BlockSpec(g.shape, c3),
    ]
    args = [rq, rk, rv, dm, qd, kd, cd, g]
    if state is not None:
        in_specs.append(pl.BlockSpec((None, None, RET_HEADS, RET_DK, RET_DV), lambda b, j: (layer, b, 0, 0, 0)))
        args.append(state)
    return pl.pallas_call(
        functools.partial(_retention_kernel, has_state=state is not None),
        grid=(n_batch, nc), in_specs=in_specs,
        out_specs=[pl.BlockSpec((c, RET_WIDTH), row),
                   pl.BlockSpec((None, RET_HEADS, RET_DK, RET_DV), lambda b, j: (b, 0, 0, 0))],
        out_shape=[jax.ShapeDtypeStruct((n, RET_WIDTH), f32),
                   jax.ShapeDtypeStruct((n_batch, RET_HEADS, RET_DK, RET_DV), f32)],
        compiler_params=_cparams(("arbitrary", "arbitrary")), name="retention",
    )(*args)


def _kth_largest(count_ge, k_sel, shape, bits, lowest):
    def value_pass(i, t):
        cand = t + lax.shift_left(jnp.int32(1), bits - 1 - i)
        return jnp.where(count_ge(cand) >= k_sel, cand, t)

    return lax.fori_loop(0, bits, value_pass, jnp.full(shape, lowest, i32))


HALF_MIN = -(2 ** 15)


def _high_half(keys):
    return (keys >> 16).astype(i16)


def _low_half(keys, t_hi):
    return jnp.where((keys >> 16) == t_hi, (keys & 0xFFFF) + HALF_MIN, HALF_MIN).astype(i16)


def _kth_largest_packed(count_hi, build_low, count_lo, k_sel, shape):
    t_hi = _kth_largest(count_hi, k_sel, shape, 16, HALF_MIN)
    n_above = jnp.where(t_hi == 2 ** 15 - 1, 0, count_hi(t_hi + 1))
    build_low(t_hi)
    t_lo = _kth_largest(count_lo, k_sel - n_above, shape, 16, HALF_MIN)
    return t_hi * 65536 + (t_lo - HALF_MIN)


def _radix_select(t, count_ge, count_eq_before, k_sel, shape, idx_bits, live=True):
    n_gt = count_ge(t + 1)
    need = k_sel - n_gt
    n_eq = count_ge(t) - n_gt
    finite = t > _NEG_INF_KEY
    cut_needed = jnp.max(jnp.where(finite & (need < n_eq) & live, 1, 0))

    def search_cut():
        def index_pass(i, c):
            cand = c + lax.shift_left(jnp.int32(1), idx_bits - 1 - i)
            return jnp.where(count_eq_before(t, cand) < need, cand, c)
        return lax.fori_loop(0, idx_bits, index_pass, jnp.zeros(shape, i32))

    c = lax.cond(cut_needed > 0, search_cut, lambda: jnp.full(shape, 2 ** idx_bits - 1, i32))
    return jnp.maximum(t, _NEG_INF_KEY), jnp.where(finite, c, -1)


def _select_mask(keys, index, t, c):
    tie = jnp.where(index <= c, 0.0, MASK_NEG)
    return jnp.where(keys > t, 0.0, jnp.where(keys == t, tie, MASK_NEG))


def _dsa_prompt_kernel(iqt_ref, iwt_ref, ik_ref, aq_ref, ak_ref, av_ref, o_ref, key_ref, hi_ref, lo_ref, bias_ref,
                       s_ref, m_ref, l_ref, acc_ref, *, k_sel, idx_bits):
    qb = pl.program_id(1)
    sub = KEY_CHUNK // Q_BLOCK
    step = functools.partial(_dsa_prompt_step, iqt_ref, iwt_ref, ik_ref, aq_ref, ak_ref, av_ref, o_ref, key_ref,
                             hi_ref, lo_ref, bias_ref, s_ref, m_ref, l_ref, acc_ref, qb, k_sel, idx_bits)
    lax.switch(qb // sub, [functools.partial(step, n + 1) for n in range(key_ref.shape[0])])


def _dsa_prompt_step(iqt_ref, iwt_ref, ik_ref, aq_ref, ak_ref, av_ref, o_ref, key_ref, hi_ref, lo_ref, bias_ref,
                     s_ref, m_ref, l_ref, acc_ref, qb, k_sel, idx_bits, nkc):
    blk, kc = Q_BLOCK, KEY_CHUNK
    sub = kc // blk

    def chunks(body, init):
        carry = init
        for c in range(nkc):
            carry = body(c, carry)
        return carry

    row_b = lax.broadcasted_iota(i32, (blk, blk), 0)
    col_b = lax.broadcasted_iota(i32, (blk, blk), 1) + qb * blk
    row_c = lax.broadcasted_iota(i32, (kc, blk), 0)
    iw = iwt_ref[...] * (IDX_DIM ** -0.5)

    def score_chunk(c, carry):
        for sb in range(sub):
            off = c * kc + sb * blk
            ikb = ik_ref[pl.ds(off, blk), :].astype(bf16)
            acc = jnp.zeros((blk, blk), f32)
            for hd in range(IDX_HEADS):
                s = jnp.dot(ikb, iqt_ref[hd * IDX_DIM:(hd + 1) * IDX_DIM, :], preferred_element_type=f32)
                acc = acc + jnp.maximum(s, 0.0) * iw[hd:hd + 1, :]
            causal = (row_b + off) <= col_b
            key = _sortable(jnp.where(causal, acc, -jnp.inf))
            key_ref[c, sb * blk:(sb + 1) * blk, :] = key
            hi_ref[c, sb * blk:(sb + 1) * blk, :] = _high_half(key)
        return carry

    chunks(score_chunk, 0)

    def count(pred):
        def body(c, acc):
            m = pred(key_ref[c], row_c + c * kc).astype(i32)
            return acc + jnp.sum(m.reshape(kc // SUBLANES, SUBLANES, blk), axis=0)
        c8 = chunks(body, jnp.zeros((SUBLANES, blk), i32))
        return jnp.sum(c8, axis=0, keepdims=True)

    def count16_ge(ref):
        packed = 2 * SUBLANES
        n_acc = 4

        def counter(cand):
            cand16 = cand.astype(i16)

            def body(c, accs):
                w = jnp.where(ref[c] >= cand16, jnp.int16(1), jnp.int16(0))
                accs = list(accs)
                for r in range(kc // packed):
                    accs[r % n_acc] = accs[r % n_acc] + w[r * packed:(r + 1) * packed]
                return tuple(accs)

            accs = chunks(body, tuple(jnp.zeros((packed, blk), i16) for _ in range(n_acc)))
            tot = (accs[0] + accs[1]) + (accs[2] + accs[3])
            return jnp.sum(tot.astype(i32), axis=0, keepdims=True)
        return counter

    def build_low(t_hi):
        for c in range(nkc):
            lo_ref[c] = _low_half(key_ref[c], t_hi)

    t = _kth_largest_packed(count16_ge(hi_ref), build_low, count16_ge(lo_ref), k_sel, (1, blk))

    count_ge = lambda cand: count(lambda kk, rows: kk >= cand)
    count_eq_before = lambda t, cand: count(lambda kk, rows: (kk == t) & (rows < cand))
    t, cut = _radix_select(t, count_ge, count_eq_before, k_sel, (1, blk), idx_bits)

    def bias_chunk(c, carry):
        for sb in range(sub):
            kk = key_ref[c, sb * blk:(sb + 1) * blk, :]
            mask = _select_mask(kk, row_b + (c * kc + sb * blk), t, cut)
            bias_ref[c, :, sb * blk:(sb + 1) * blk] = mask.T
        return carry

    chunks(bias_chunk, 0)

    def lane_fold(x, op):
        y = x[:, :blk]
        for sb in range(1, sub):
            y = op(y, x[:, sb * blk:(sb + 1) * blk])
        return y

    m_ref[...] = jnp.full(m_ref.shape, MASK_NEG, f32)

    def qk_chunk(c, carry):
        off = c * kc
        bias = bias_ref[c]
        for hd in range(ATT_HEADS):
            sl = slice(hd * HEAD_DIM, (hd + 1) * HEAD_DIM)
            kh = ak_ref[pl.ds(off, kc), sl]
            s = lax.dot_general(aq_ref[:, sl], kh, _NT, preferred_element_type=f32) * (HEAD_DIM ** -0.5) + bias
            s_ref[c, hd] = s
            m_ref[hd] = jnp.maximum(m_ref[hd], lane_fold(s, jnp.maximum))
        return carry

    chunks(qk_chunk, 0)
    m_fin = [jnp.max(m_ref[hd], axis=-1, keepdims=True) for hd in range(ATT_HEADS)]
    l_ref[...] = jnp.zeros(l_ref.shape, f32)
    acc_ref[...] = jnp.zeros(acc_ref.shape, f32)

    def pv_chunk(c, carry):
        off = c * kc
        for hd in range(ATT_HEADS):
            sl = slice(hd * HEAD_DIM, (hd + 1) * HEAD_DIM)
            p = jnp.exp(s_ref[c, hd] - m_fin[hd])
            l_ref[hd] = l_ref[hd] + lane_fold(p, jnp.add)
            acc_ref[hd] = acc_ref[hd] + jnp.dot(p.astype(bf16), av_ref[pl.ds(off, kc), sl], preferred_element_type=f32)
        return carry

    chunks(pv_chunk, 0)
    for hd in range(ATT_HEADS):
        o_ref[:, hd * HEAD_DIM:(hd + 1) * HEAD_DIM] = acc_ref[hd] / jnp.sum(l_ref[hd], axis=-1, keepdims=True)
    return 0


def _dsa_prompt(iqt, iwt, ik, aq, ak, av, n_batch):
    n = aq.shape[0]
    t = n // n_batch
    assert t % KEY_CHUNK == 0, t
    nqb = t // Q_BLOCK
    nkc = t // KEY_CHUNK
    k_sel = min(TOPK_MAX, t // 4)
    idx_bits = max(1, (t - 1).bit_length())
    qrow = lambda b, q: (b * nqb + q, 0)
    qcol = lambda b, q: (0, b * nqb + q)
    brow = lambda b, q: (b, 0)
    return pl.pallas_call(
        functools.partial(_dsa_prompt_kernel, k_sel=k_sel, idx_bits=idx_bits),
        grid=(n_batch, nqb),
        in_specs=[
            pl.BlockSpec((IDX_HEADS * IDX_DIM, Q_BLOCK), qcol),
            pl.BlockSpec((IDX_HEADS, Q_BLOCK), qcol),
            pl.BlockSpec((t, IDX_DIM), brow),
            pl.BlockSpec((Q_BLOCK, ATT_WIDTH), qrow),
            pl.BlockSpec((t, ATT_WIDTH), brow),
            pl.BlockSpec((t, ATT_WIDTH), brow),
        ],
        out_specs=pl.BlockSpec((Q_BLOCK, ATT_WIDTH), qrow),
        out_shape=jax.ShapeDtypeStruct((n, ATT_WIDTH), f32),
        scratch_shapes=[pltpu.VMEM((nkc, KEY_CHUNK, Q_BLOCK), i32), pltpu.VMEM((nkc, KEY_CHUNK, Q_BLOCK), i16),
                        pltpu.VMEM((nkc, KEY_CHUNK, Q_BLOCK), i16), pltpu.VMEM((nkc, Q_BLOCK, KEY_CHUNK), f32),
                        pltpu.VMEM((nkc, ATT_HEADS, Q_BLOCK, KEY_CHUNK), f32),
                        pltpu.VMEM((ATT_HEADS, Q_BLOCK, Q_BLOCK), f32),
                        pltpu.VMEM((ATT_HEADS, Q_BLOCK, Q_BLOCK), f32),
                        pltpu.VMEM((ATT_HEADS, Q_BLOCK, HEAD_DIM), f32)],
        compiler_params=_cparams(("arbitrary", "arbitrary")), name="dsa_prompt",
    )(iqt, iwt, ik, aq, ak, av)


def _page_scores(iq, iw, ik_page):
    s = jnp.dot(iq, ik_page.astype(bf16), preferred_element_type=f32)
    s = jnp.maximum(s * (IDX_DIM ** -0.5), 0.0) * iw
    return jnp.sum(s.reshape(IDX_HEADS, SAMPLE_Q, PAGE_SIZE), axis=0)


def _idx_scores_kernel(pt_ref, iq_ref, iw_ref, *refs, pages):
    page_refs, o_ref = refs[:pages], refs[pages]
    iq, iw = iq_ref[...], iw_ref[...]
    for i in range(pages):
        o_ref[i] = _page_scores(iq, iw, page_refs[i][...])


def _idx_scores(page_table, iq_s, iw_s, kidx_t, layer, pages):
    db, n_pages = page_table.shape
    nj = n_pages // pages
    page_spec = lambda i: pl.BlockSpec((None, None, IDX_DIM, PAGE_SIZE),
                                       lambda b, j, pt: (layer, pt[b, j * pages + i], 0, 0))
    return pl.pallas_call(
        functools.partial(_idx_scores_kernel, pages=pages),
        grid_spec=pltpu.PrefetchScalarGridSpec(
            num_scalar_prefetch=1, grid=(db, nj),
            in_specs=[pl.BlockSpec((None, IDX_HEADS * SAMPLE_Q, IDX_DIM), lambda b, j, pt: (b, 0, 0)),
                      pl.BlockSpec((None, IDX_HEADS * SAMPLE_Q, LANES), lambda b, j, pt: (b, 0, 0))]
                     + [page_spec(i) for i in range(pages)],
            out_specs=pl.BlockSpec((None, pages, SAMPLE_Q, PAGE_SIZE), lambda b, j, pt: (b, j, 0, 0)),
        ),
        out_shape=jax.ShapeDtypeStruct((db, n_pages, SAMPLE_Q, PAGE_SIZE), f32),
        compiler_params=_cparams(("arbitrary", "arbitrary")), name="idx_scores",
    )(page_table, iq_s, iw_s, *([kidx_t] * pages))


def _select_kernel(sc_ref, iq_ref, iw_ref, ikn_ref, bias_ref, key_ref, hi_ref, lo_ref, *, k_sel, idx_bits, n_true):
    n_req, n_pages = sc_ref.shape[0], sc_ref.shape[1]
    rows = n_req * SAMPLE_Q
    nch = n_pages + 1
    unroll = 8 if n_pages % 8 == 0 else 1
    lane_i = lax.broadcasted_iota(i32, (rows, LANES), 1)
    q_i = lax.broadcasted_iota(i32, (SAMPLE_Q, LANES), 0)
    l_i = lax.broadcasted_iota(i32, (SAMPLE_Q, LANES), 1)

    def to_key(ch, carry):
        key = _sortable(jnp.concatenate([sc_ref[r, ch] for r in range(n_req)], axis=0))
        key_ref[ch] = key
        hi_ref[ch] = _high_half(key)
        return carry

    lax.fori_loop(0, n_pages, to_key, 0)
    new = []
    for r in range(n_req):
        s = _page_scores(iq_ref[r], iw_ref[r], ikn_ref[r])
        new.append(jnp.where(l_i <= q_i, s, -jnp.inf))
    key_new = _sortable(jnp.concatenate(new, axis=0))
    key_ref[n_pages] = key_new
    hi_ref[n_pages] = _high_half(key_new)

    def count16_ge(ref):
        def counter(cand):
            cand16 = jnp.broadcast_to(cand, (rows, LANES)).astype(i16)
            hit = lambda ch: jnp.where(ref[ch] >= cand16, jnp.int16(1), jnp.int16(0))

            def body(g, accs):
                accs = list(accs)
                for u in range(unroll):
                    accs[u % 2] = accs[u % 2] + hit(g * unroll + u)
                return tuple(accs)

            zero = jnp.zeros((rows, LANES), i16)
            a0, a1 = lax.fori_loop(0, n_pages // unroll, body, (zero, zero))
            tot = (a0 + a1) + hit(n_pages)
            return jnp.sum(tot.astype(i32), axis=-1, keepdims=True)
        return counter

    def build_low(t_hi):
        t_hi_b = jnp.broadcast_to(t_hi, (rows, LANES))

        def body(ch, carry):
            lo_ref[ch] = _low_half(key_ref[ch], t_hi_b)
            return carry
        lax.fori_loop(0, nch, body, 0)

    def count_ge(cand):
        cand_b = jnp.broadcast_to(cand, (rows, LANES))

        def body(ch, c):
            return c + (key_ref[ch] >= cand_b).astype(i32)
        c = lax.fori_loop(0, nch, body, jnp.zeros((rows, LANES), i32))
        return jnp.sum(c, axis=-1, keepdims=True)

    def count_eq_before(t, cand):
        t_b = jnp.broadcast_to(t, (rows, LANES))
        cand_b = jnp.broadcast_to(cand, (rows, LANES))

        def body(ch, c):
            return c + ((key_ref[ch] == t_b) & ((lane_i + ch * LANES) < cand_b)).astype(i32)
        c = lax.fori_loop(0, nch, body, jnp.zeros((rows, LANES), i32))
        return jnp.sum(c, axis=-1, keepdims=True)

    t = _kth_largest_packed(count16_ge(hi_ref), build_low, count16_ge(lo_ref), k_sel, (rows, 1))
    live = lax.broadcasted_iota(i32, (rows, 1), 0) % SAMPLE_Q < n_true
    t, c = _radix_select(t, count_ge, count_eq_before, k_sel, (rows, 1), idx_bits, live)
    t_b = jnp.broadcast_to(t, (rows, LANES))
    c_b = jnp.broadcast_to(c, (rows, LANES))

    def to_bias(ch, carry):
        bias = _select_mask(key_ref[ch], lane_i + ch * LANES, t_b, c_b)
        for r in range(n_req):
            bias_ref[r, ch] = bias[r * SAMPLE_Q:(r + 1) * SAMPLE_Q]
        return carry

    lax.fori_loop(0, nch, to_bias, 0)


def _select(scores, iq_s, iw_s, ik_new, k_sel, n_req, n_true):
    db, n_pages = scores.shape[0], scores.shape[1]
    nch = n_pages + 1
    idx_bits = max(1, (nch * LANES - 1).bit_length())
    req3 = lambda g: (g, 0, 0)
    req4 = lambda g: (g, 0, 0, 0)
    return pl.pallas_call(
        functools.partial(_select_kernel, k_sel=k_sel, idx_bits=idx_bits, n_true=n_true),
        grid=(db // n_req,),
        in_specs=[pl.BlockSpec((n_req, n_pages, SAMPLE_Q, LANES), req4),
                  pl.BlockSpec((n_req, IDX_HEADS * SAMPLE_Q, IDX_DIM), req3),
                  pl.BlockSpec((n_req, IDX_HEADS * SAMPLE_Q, LANES), req3),
                  pl.BlockSpec((n_req, IDX_DIM, PAGE_SIZE), req3)],
        out_specs=pl.BlockSpec((n_req, nch, SAMPLE_Q, LANES), req4),
        out_shape=jax.ShapeDtypeStruct((db, nch, SAMPLE_Q, LANES), f32),
        scratch_shapes=[pltpu.VMEM((nch, n_req * SAMPLE_Q, LANES), i32),
                        pltpu.VMEM((nch, n_req * SAMPLE_Q, LANES), i16),
                        pltpu.VMEM((nch, n_req * SAMPLE_Q, LANES), i16)],
        compiler_params=_cparams(("arbitrary",)), name="select",
    )(scores, iq_s, iw_s, ik_new)


def _paged_attn_kernel(pt_ref, q_ref, hm_ref, ex_ref, bias_ref, biasn_ref, kn_ref, vn_ref, *refs, pages):
    k_refs, v_refs = refs[:pages], refs[pages:2 * pages]
    o_ref, q_s, m_ref, l_ref, acc_ref = refs[2 * pages:]
    j = pl.program_id(1)
    rows = ATT_HEADS * SAMPLE_Q

    @pl.when(j == 0)
    def _():
        qf = q_ref[...].astype(f32)
        q_s[...] = jnp.concatenate(
            [qf[:SAMPLE_Q, hd * HEAD_DIM:(hd + 1) * HEAD_DIM] for hd in range(ATT_HEADS)], axis=0).astype(bf16)
        m_ref[...] = jnp.full(m_ref.shape, MASK_NEG, f32)
        l_ref[...] = jnp.zeros(l_ref.shape, f32)
        acc_ref[...] = jnp.zeros(acc_ref.shape, f32)

    q = q_s[...]
    hm = hm_ref[...]

    def scores(k_page, b8):
        s = lax.dot_general(q, k_page.astype(bf16), _NT, preferred_element_type=f32) * (HEAD_DIM ** -0.5)
        sel = jnp.where(b8 == 0.0, 1.0, 0.0)
        sel = jnp.concatenate([sel] * ATT_HEADS, axis=0).astype(bf16)
        selx = jnp.dot(sel, ex_ref[...], preferred_element_type=f32) * hm
        return jnp.where(selx > 0.5, s, MASK_NEG)

    def accumulate(s_list, v_list):
        m_old = m_ref[...]
        mx = s_list[0]
        for s in s_list[1:]:
            mx = jnp.maximum(mx, s)
        m_new = jnp.maximum(m_old, jnp.max(mx, axis=-1, keepdims=True))
        alpha = jnp.exp(m_old - m_new)
        psum = jnp.zeros((rows, ATT_HEADS * PAGE_SIZE), f32)
        pv = jnp.zeros((rows, HEAD_DIM), f32)
        for s, v in zip(s_list, v_list):
            p = jnp.exp(s - m_new)
            psum = psum + p
            pv = pv + jnp.dot(p.astype(bf16), v.astype(bf16), preferred_element_type=f32)
        l_ref[...] = alpha * l_ref[...] + jnp.sum(psum, axis=-1, keepdims=True)
        acc_ref[...] = alpha * acc_ref[...] + pv
        m_ref[...] = m_new

    accumulate([scores(k_refs[i][...], bias_ref[i]) for i in range(pages)],
               [v_refs[i][...] for i in range(pages)])

    @pl.when(j == pl.num_programs(1) - 1)
    def _():
        accumulate([scores(kn_ref[...], biasn_ref[...])], [vn_ref[...]])
        o = acc_ref[...] / l_ref[...]
        o_ref[...] = jnp.zeros(o_ref.shape, f32)
        for hd in range(ATT_HEADS):
            o_ref[:SAMPLE_Q, hd * HEAD_DIM:(hd + 1) * HEAD_DIM] = o[hd * SAMPLE_Q:(hd + 1) * SAMPLE_Q]


def _paged_attn(page_table, aq_s, bias, k_new, v_new, cache_k, cache_v, layer, pages):
    db, n_pages = page_table.shape
    nj = n_pages // pages
    page_rows = PAGE_SIZE * ATT_HEADS
    rows = ATT_HEADS * SAMPLE_Q
    col = jnp.arange(page_rows)
    head_match = (col[None, :] % ATT_HEADS == jnp.arange(rows)[:, None] // SAMPLE_Q).astype(f32)
    expand = (col[None, :] // ATT_HEADS == jnp.arange(PAGE_SIZE)[:, None]).astype(bf16)
    page_spec = lambda i: pl.BlockSpec((None, None, page_rows, HEAD_DIM),
                                       lambda b, j, pt: (layer, pt[b, j * pages + i], 0, 0))
    req = lambda b, j, pt: (b, 0)
    const = lambda b, j, pt: (0, 0)
    return pl.pallas_call(
        functools.partial(_paged_attn_kernel, pages=pages),
        grid_spec=pltpu.PrefetchScalarGridSpec(
            num_scalar_prefetch=1, grid=(db, nj),
            in_specs=[pl.BlockSpec((SAMPLE_ROWS, ATT_WIDTH), req),
                      pl.BlockSpec(head_match.shape, const),
                      pl.BlockSpec(expand.shape, const),
                      pl.BlockSpec((None, pages, SAMPLE_Q, LANES), lambda b, j, pt: (b, j, 0, 0)),
                      pl.BlockSpec((None, None, SAMPLE_Q, LANES), lambda b, j, pt: (b, n_pages, 0, 0)),
                      pl.BlockSpec((page_rows, HEAD_DIM), req),
                      pl.BlockSpec((page_rows, HEAD_DIM), req)]
                     + [page_spec(i) for i in range(pages)] * 2,
            out_specs=pl.BlockSpec((SAMPLE_ROWS, ATT_WIDTH), req),
            scratch_shapes=[pltpu.VMEM((rows, HEAD_DIM), bf16),
                            pltpu.VMEM((rows, 1), f32),
                            pltpu.VMEM((rows, 1), f32),
                            pltpu.VMEM((rows, HEAD_DIM), f32)],
        ),
        out_shape=jax.ShapeDtypeStruct((db * SAMPLE_ROWS, ATT_WIDTH), f32),
        compiler_params=_cparams(("arbitrary", "arbitrary")), name="paged_attn",
    )(page_table, aq_s, head_match, expand, bias, bias, k_new, v_new, *([cache_k] * pages), *([cache_v] * pages))


def _merge_kernel(x_ref, oret_ref, oatt_ref, ng_ref, wg_ref, wbr_ref, wba_ref, wo_ref, y_ref):
    x = x_ref[...]
    ms = jnp.mean(x * x, axis=-1, keepdims=True)
    h = ((x * lax.rsqrt(ms + EPS)) * ng_ref[...]).astype(bf16)
    g = jnp.dot(h, wg_ref[...], preferred_element_type=f32)
    rg = g[:, 0:RET_WIDTH]
    ag = g[:, RET_WIDTH:RET_WIDTH + ATT_WIDTH]
    ga = g[:, RET_WIDTH + ATT_WIDTH:RET_WIDTH + ATT_WIDTH + D_MODEL]
    gb = g[:, RET_WIDTH + ATT_WIDTH + D_MODEL:]
    o_ret = oret_ref[...] * (rg * jax.nn.sigmoid(rg))
    o_att = oatt_ref[...] * (ag * jax.nn.sigmoid(ag))
    a = jnp.dot(o_ret.astype(bf16), wbr_ref[...], preferred_element_type=f32)
    b = jnp.dot(o_att.astype(bf16), wba_ref[...], preferred_element_type=f32)
    y = jax.nn.sigmoid(ga) * a + jax.nn.sigmoid(gb) * b
    y_ref[...] = x + jnp.dot(y.astype(bf16), wo_ref[...], preferred_element_type=f32)


def _merge(x2d, o_ret, o_att, w, tm):
    n = x2d.shape[0]
    row = lambda i: (i, 0)
    full = lambda a: pl.BlockSpec(a.shape, lambda i: (0, 0))
    return pl.pallas_call(
        _merge_kernel, grid=(n // tm,),
        in_specs=[pl.BlockSpec((tm, D_MODEL), row), pl.BlockSpec((tm, RET_WIDTH), row),
                  pl.BlockSpec((tm, ATT_WIDTH), row),
                  full(w["ng"]), full(w["wg"]), full(w["wbr"]), full(w["wba"]), full(w["wo"])],
        out_specs=pl.BlockSpec((tm, D_MODEL), row),
        out_shape=jax.ShapeDtypeStruct((n, D_MODEL), f32),
        compiler_params=_cparams(("arbitrary",)), name="merge",
    )(x2d, o_ret, o_att, w["ng"], w["wg"], w["wbr"], w["wba"], w["wo"])


def _prep_weights(l, norm_g, w_in, q_norm_g, k_norm_g, kidx_norm_g, w_br_ret, w_br_att, w_out):
    w = w_in[l]
    pad_cols = lambda a, n: jnp.pad(a, ((0, 0), (0, n - a.shape[1])))
    wiq = jnp.concatenate([w[:, _O_IQ:_O_IK], w[:, _O_IW:_O_GA]], axis=1).T
    wiq = jnp.pad(wiq, ((0, 2 * SUBLANES - IDX_HEADS), (0, 0)))
    return {
        "ng": norm_g[l].reshape(1, D_MODEL),
        "wr": w[:, _O_RQ:_O_RG].astype(bf16),
        "wa": jnp.concatenate([w[:, _O_AQ:_O_AG]], axis=1).astype(bf16),
        "wiq": wiq.astype(bf16),
        "wik": pad_cols(w[:, _O_IK:_O_IW], LANES).astype(bf16),
        "qg": q_norm_g[l].reshape(1, HEAD_DIM),
        "kg": k_norm_g[l].reshape(1, HEAD_DIM),
        "ig": pad_cols(kidx_norm_g[l].reshape(1, IDX_DIM), LANES),
        "wg": jnp.concatenate([w[:, _O_RG:_O_AQ], w[:, _O_AG:_O_IQ], w[:, _O_GA:_O_END]], axis=1).astype(bf16),
        "wbr": w_br_ret[l].astype(bf16),
        "wba": w_br_att[l].astype(bf16),
        "wo": w_out[l].astype(bf16),
    }


def _position_tables(pos):
    rc, rs, _, _ = _rope_tables(pos, RET_DK, RET_ROPE_THETA, LANES)
    ac, as_, _, _ = _rope_tables(pos, ROPE_DIM, ROPE_THETA, LANES)
    ic, is_, ict, ist = _rope_tables(pos, IDX_ROPE_DIM, ROPE_THETA, LANES)
    return rc, rs, ac, as_, ic, is_, ict, ist


def _token_tile(n):
    for tm in (256, 128):
        if n % tm == 0:
            return tm
    assert n % SAMPLE_ROWS == 0 and n < LANES, n
    return n


def _prompt_layer(x2d, n_batch, w, ret_g, tabs, tm):
    rq, rk, rv, aq, ak, av, akb, avb, iqt, ik, iwt = _inproj(x2d, w, tabs, tm)
    o_ret, s_new = _retention(rq, rk, rv, ret_g, n_batch, RET_CHUNK)
    o_att = _dsa_prompt(iqt, iwt, ik, aq, akb, avb, n_batch)
    y = _merge(x2d, o_ret, o_att, w, tm)
    return y, ak, av, ik, s_new


def _sample_layer(xs, n_true, w, ret_g, tabs, tm, layer, state_ret, page_table, cache_k, cache_v, kidx_t, pages):
    db = xs.shape[0]
    n = db * SAMPLE_ROWS
    past = page_table.shape[1] * PAGE_SIZE
    x2d = xs.reshape(n, D_MODEL)
    rq, rk, rv, aq, ak, av, _, _, iqt, ik, iwt = _inproj(x2d, w, tabs, tm)

    padc = lambda a: jnp.pad(a.reshape(db, SAMPLE_ROWS, -1), ((0, 0), (0, RET_CHUNK - SAMPLE_ROWS), (0, 0))
                             ).reshape(db * RET_CHUNK, -1)
    o_ret, s_new = _retention(padc(rq), padc(rk), padc(rv), ret_g, db, n_true, state=state_ret, layer=layer)
    o_ret = o_ret.reshape(db, RET_CHUNK, RET_WIDTH)[:, :SAMPLE_ROWS].reshape(n, RET_WIDTH)

    iq_s = iqt.reshape(IDX_HEADS, IDX_DIM, db, SAMPLE_ROWS)[:, :, :, :SAMPLE_Q]
    iq_s = iq_s.transpose(2, 0, 3, 1).reshape(db, IDX_HEADS * SAMPLE_Q, IDX_DIM)
    iw_s = iwt.reshape(IDX_HEADS, db, SAMPLE_ROWS)[:, :, :SAMPLE_Q].transpose(1, 0, 2)
    iw_s = jnp.broadcast_to(iw_s.reshape(db, IDX_HEADS * SAMPLE_Q, 1), (db, IDX_HEADS * SAMPLE_Q, LANES))
    ik_new = jnp.pad(ik.reshape(db, SAMPLE_ROWS, IDX_DIM), ((0, 0), (0, PAGE_SIZE - SAMPLE_ROWS), (0, 0)))
    ik_new = ik_new.transpose(0, 2, 1)

    idx_pages = 2 * pages if page_table.shape[1] % (2 * pages) == 0 else pages
    scores = _idx_scores(page_table, iq_s, iw_s, kidx_t, layer, idx_pages)
    k_sel = min(TOPK_MAX, (past + n_true) // 4)
    n_req = 8 if db % 8 == 0 else db
    bias = _select(scores, iq_s, iw_s, ik_new, k_sel, n_req, n_true)

    pad_new = lambda a: jnp.pad(a.reshape(db, SAMPLE_ROWS * ATT_HEADS, HEAD_DIM),
                                ((0, 0), (0, (PAGE_SIZE - SAMPLE_ROWS) * ATT_HEADS), (0, 0))
                                ).reshape(db * PAGE_SIZE * ATT_HEADS, HEAD_DIM)
    o_att = _paged_attn(page_table, aq, bias, pad_new(ak), pad_new(av), cache_k, cache_v, layer, pages)
    y = _merge(x2d, o_ret, o_att, w, tm)
    return y.reshape(db, SAMPLE_ROWS, D_MODEL), ak, av, ik, s_new


def kernel(x_prompt, x_sample, cache_k, cache_v, cache_kidx, state_ret, page_table, norm_g, w_in, q_norm_g,
           k_norm_g, kidx_norm_g, ret_norm_g, w_br_ret, w_br_att, w_out):
    depth = w_in.shape[0]
    n_batch, t, _ = x_prompt.shape
    db, ds, _ = x_sample.shape
    n_pages = page_table.shape[1]
    past = n_pages * PAGE_SIZE
    assert t % RET_CHUNK == 0 and ds <= SAMPLE_Q

    tm_p = _token_tile(t)
    tabs_p = _position_tables(jnp.arange(t))
    n_s = db * SAMPLE_ROWS
    tm_s = _token_tile(n_s)
    pos_s = past + (jnp.arange(tm_s) % SAMPLE_ROWS)
    tabs_s = _position_tables(pos_s)
    pages = 16 if n_pages % 16 == 0 else n_pages

    n_pool = cache_k.shape[1]
    cache_k = cache_k.reshape(depth, n_pool, PAGE_SIZE * ATT_HEADS, HEAD_DIM)
    cache_v = cache_v.reshape(depth, n_pool, PAGE_SIZE * ATT_HEADS, HEAD_DIM)
    kidx_t = jnp.swapaxes(cache_kidx, 2, 3)

    xp = x_prompt.reshape(n_batch * t, D_MODEL)
    pad_s = lambda a: jnp.pad(a, ((0, 0), (0, SAMPLE_ROWS - ds), (0, 0)))
    xs = pad_s(x_sample)
    kp, vp, ip, sp, kd, vd, idd, sd = [], [], [], [], [], [], [], []
    for l in range(depth):
        w = _prep_weights(l, norm_g, w_in, q_norm_g, k_norm_g, kidx_norm_g, w_br_ret, w_br_att, w_out)
        xp, k_new, v_new, i_new, s_new = _prompt_layer(xp, n_batch, w, ret_norm_g[l], tabs_p, tm_p)
        kp.append(k_new.reshape(n_batch, t, ATT_HEADS, HEAD_DIM))
        vp.append(v_new.reshape(n_batch, t, ATT_HEADS, HEAD_DIM))
        ip.append(i_new.reshape(n_batch, t, IDX_DIM))
        sp.append(s_new)
        ys, k_new, v_new, i_new, s_new = _sample_layer(
            xs, ds, w, ret_norm_g[l], tabs_s, tm_s, l, state_ret, page_table, cache_k, cache_v, kidx_t, pages)
        xs = pad_s(ys[:, :ds])
        kd.append(k_new.reshape(db, SAMPLE_ROWS, ATT_HEADS, HEAD_DIM)[:, :ds])
        vd.append(v_new.reshape(db, SAMPLE_ROWS, ATT_HEADS, HEAD_DIM)[:, :ds])
        idd.append(i_new.reshape(db, SAMPLE_ROWS, IDX_DIM)[:, :ds])
        sd.append(s_new)
    return (xp.reshape(n_batch, t, D_MODEL), xs[:, :ds],
            jnp.stack(kp), jnp.stack(vp), jnp.stack(ip), jnp.stack(sp),
            jnp.stack(kd), jnp.stack(vd), jnp.stack(idd), jnp.stack(sd))
```
